```python
import math
import jax, jax.numpy as jnp
from jax import lax
import numpy as np

D_MODEL = 1024
BATCH = 2
SEQ = 8192
DEPTH = 1

ATT_HEAD_DIM = 64
ATT_HEADS_PER_GROUP = 8
ATT_GROUPS = ((128, 1), (512, 4), (2048, 16))
ATT_N_HEADS = ATT_HEADS_PER_GROUP * len(ATT_GROUPS)
ATT_WIDTH = ATT_N_HEADS * ATT_HEAD_DIM
ATT_OUT_WIDTH = ATT_HEADS_PER_GROUP * ATT_HEAD_DIM
ATT_BLOCK = 128
ROPE_THETA = 500000.0
ROPE_DIM = ATT_HEAD_DIM // 4
HGRN_EXPAND = 128
HGRN_HEADS = D_MODEL // HGRN_EXPAND
HGRN_KEY_DIM = HGRN_EXPAND
HGRN_VAL_DIM = D_MODEL // HGRN_HEADS
HGRN_WIDTH = HGRN_HEADS * HGRN_KEY_DIM
HGRN_V_WIDTH = HGRN_HEADS * HGRN_VAL_DIM
HGRN_CHUNK = 64
N_BRANCH = 2
MIX_IN_WIDTH = 3 * ATT_WIDTH + 2 * HGRN_WIDTH + 2 * HGRN_V_WIDTH + N_BRANCH * D_MODEL
D_FF = 2816
PLE_DIM = 256
ALPHA = (2 * DEPTH) ** 0.25
BETA = (8 * DEPTH) ** -0.25
LN_EPS = 1e-5
RMS_EPS = 1e-6

kernel_name = "hybrid_dilated_attn_hgrn2_macaron_deepnorm"


def _layer_norm(x, g, b):
    xf = x.astype(jnp.float32)
    mu = jnp.mean(xf, axis=-1, keepdims=True)
    var = jnp.mean(jnp.square(xf - mu), axis=-1, keepdims=True)
    y = (xf - mu) * lax.rsqrt(var + LN_EPS) * g.astype(jnp.float32) + b.astype(jnp.float32)
    return y.astype(x.dtype)


def _swiglu(x, w_in, w_out):
    a, u = jnp.split(x @ w_in, 2, axis=-1)
    return (jax.nn.silu(a) * u) @ w_out


def _partial_rope(x, positions):
    half = ROPE_DIM // 2
    inv_freq = jnp.exp(-math.log(ROPE_THETA) * jnp.arange(half, dtype=jnp.float32) * (2.0 / ROPE_DIM))
    ang = positions.astype(jnp.float32)[:, :, None] * inv_freq
    cos = jnp.cos(ang)[:, :, None, :]
    sin = jnp.sin(ang)[:, :, None, :]
    xr = x[..., :ROPE_DIM].astype(jnp.float32)
    x1, x2 = xr[..., :half], xr[..., half:]
    rot = jnp.concatenate([x1 * cos - x2 * sin, x2 * cos + x1 * sin], axis=-1).astype(x.dtype)
    return jnp.concatenate([rot, x[..., ROPE_DIM:]], axis=-1)


def _dilated_window_attention(q, k, v, window, dilation):
    b_, s_, h_, dh = q.shape
    sub_len = s_ // dilation
    w_sub = window // dilation
    blk = math.gcd(ATT_BLOCK, sub_len)
    nb = sub_len // blk

    def to_sub(t):
        return t.reshape(b_, sub_len, dilation, h_, dh).transpose(0, 2, 3, 1, 4)

    qs = (to_sub(q) * (dh ** -0.5)).reshape(b_, dilation, h_, nb, blk, dh)
    pad = ((0, 0), (0, 0), (0, 0), (w_sub, 0), (0, 0))
    ks = jnp.pad(to_sub(k), pad)
    vs = jnp.pad(to_sub(v), pad)
    kidx = jnp.arange(nb)[:, None] * blk + jnp.arange(blk + w_sub)[None, :]
    kb = ks[:, :, :, kidx, :]
    vb = vs[:, :, :, kidx, :]
    kpos = kidx - w_sub
    qpos = jnp.arange(nb)[:, None] * blk + jnp.arange(blk)[None, :]
    dist = qpos[:, :, None] - kpos[:, None, :]
    valid = (dist >= 0) & (dist <= w_sub) & (kpos[:, None, :] >= 0)
    s = jnp.einsum('bdhnqc,bdhnkc->bdhnqk', qs, kb, preferred_element_type=jnp.float32)
    s = jnp.where(valid, s, -jnp.inf)
    m = jnp.max(s, axis=-1, keepdims=True)
    e = jnp.exp(s - m)
    l = jnp.sum(e, axis=-1, keepdims=True)
    o = jnp.einsum('bdhnqk,bdhnkc->bdhnqc', e, vb.astype(jnp.float32)) / l
    lse = (m + jnp.log(l))[..., 0]
    o = o.reshape(b_, dilation, h_, sub_len, dh).transpose(0, 3, 1, 2, 4).reshape(b_, s_, h_, dh)
    lse = lse.reshape(b_, dilation, h_, sub_len).transpose(0, 3, 1, 2).reshape(b_, s_, h_)
    return o, lse


def _hgrn2_chunk_scan(q, k, v, log_f):
    b_, s_, h_, dk = q.shape
    dv = v.shape[-1]
    c = HGRN_CHUNK
    nc = s_ // c

    def chunks(t):
        return t.reshape(b_, nc, c, h_, t.shape[-1]).transpose(1, 0, 3, 2, 4)

    qc, kc, vc = chunks(q), chunks(k), chunks(v)
    bc = jnp.cumsum(chunks(log_f), axis=-2)
    causal = jnp.tril(jnp.ones((c, c), dtype=bool))[:, :, None]

    def step(state, inp):
        q_, k_, v_, b_c = inp
        diff = b_c[:, :, :, None, :] - b_c[:, :, None, :, :]
        decay = jnp.where(causal, jnp.exp(jnp.where(causal, diff, 0.0)), 0.0)
        att = jnp.einsum('bhtc,bhtsc,bhsc->bhts', q_, decay, k_)
        o = jnp.einsum('bhts,bhsv->bhtv', att, v_)
        o = o + jnp.einsum('bhtc,bhcv->bhtv', q_ * jnp.exp(b_c), state)
        b_last = b_c[:, :, -1:, :]
        state = (jnp.exp(b_last[:, :, 0, :])[..., None] * state
                 + jnp.einsum('bhsc,bhsv->bhcv', k_ * jnp.exp(b_last - b_c), v_))
        return state, o

    state0 = jnp.zeros((b_, h_, dk, dv), jnp.float32)
    _, o = lax.scan(step, state0, (qc, kc, vc, bc))
    return o.transpose(1, 0, 3, 2, 4).reshape(b_, s_, h_, dv)


def _token_mix(h, positions, w_in, lower_bound, norm_g, w_pa, w_pb, w_o):
    b_, s_, _ = h.shape
    z = h @ w_in
    sizes = [ATT_WIDTH, ATT_WIDTH, ATT_WIDTH, HGRN_WIDTH, HGRN_WIDTH, HGRN_V_WIDTH, HGRN_V_WIDTH]
    cuts = [int(c) for c in np.cumsum(sizes)]
    qa, ka, va, qh, fh, ih, gh, gates = jnp.split(z, cuts, axis=-1)

    qa = _partial_rope(qa.reshape(b_, s_, ATT_N_HEADS, ATT_HEAD_DIM), positions)
    ka = _partial_rope(ka.reshape(b_, s_, ATT_N_HEADS, ATT_HEAD_DIM), positions)
    va = va.reshape(b_, s_, ATT_N_HEADS, ATT_HEAD_DIM)
    outs, lses = [], []
    for gi, (window, dilation) in enumerate(ATT_GROUPS):
        sl = slice(gi * ATT_HEADS_PER_GROUP, (gi + 1) * ATT_HEADS_PER_GROUP)
        o_g, lse_g = _dilated_window_attention(qa[:, :, sl], ka[:, :, sl], va[:, :, sl], window, dilation)
        outs.append(o_g)
        lses.append(lse_g)
    wts = jax.nn.softmax(jnp.stack(lses, axis=0), axis=0)
    o_att = jnp.einsum('gbsh,gbshc->bshc', wts, jnp.stack(outs, axis=0))
    y_att = o_att.reshape(b_, s_, ATT_OUT_WIDTH).astype(h.dtype) @ w_pa

    q = jax.nn.silu(qh.astype(jnp.float32)).reshape(b_, s_, HGRN_HEADS, HGRN_KEY_DIM)
    lb = lower_bound.astype(jnp.float32).reshape(HGRN_HEADS, HGRN_KEY_DIM)
    f = lb + (1.0 - lb) * jax.nn.sigmoid(fh.astype(jnp.float32)).reshape(b_, s_, HGRN_HEADS, HGRN_KEY_DIM)
    v = ih.astype(jnp.float32).reshape(b_, s_, HGRN_HEADS, HGRN_VAL_DIM)
    o = _hgrn2_chunk_scan(q, 1.0 - f, v, jnp.log(f))
    o = o * lax.rsqrt(jnp.mean(jnp.square(o), axis=-1, keepdims=True) + RMS_EPS)
    o = o * norm_g.astype(jnp.float32).reshape(HGRN_HEADS, HGRN_VAL_DIM)
    o = o.reshape(b_, s_, HGRN_V_WIDTH) * jax.nn.silu(gh.astype(jnp.float32))
    y_hgrn = o.astype(h.dtype) @ w_pb

    g_att, g_hgrn = jnp.split(jax.nn.sigmoid(gates), N_BRANCH, axis=-1)
    return (g_att * y_att + g_hgrn * y_hgrn) @ w_o


def setup_inputs(seed: int = 0) -> dict:
    key = jax.random.key(seed)
    ks = jax.random.split(key, 24)
    f32 = jnp.float32

    def nrm(k, shape, scale):
        return jax.random.normal(k, shape, f32) * scale

    def gain(k, shape):
        return 1.0 + 0.05 * jax.random.normal(k, shape, f32)

    def bias(k, shape):
        return 0.02 * jax.random.normal(k, shape, f32)

    x = jax.random.normal(ks[0], (BATCH, SEQ, D_MODEL), f32)
    p = jax.random.normal(ks[1], (DEPTH, BATCH, SEQ, PLE_DIM), f32)
    start = jax.random.randint(ks[2], (BATCH, 1), 0, 4096, dtype=jnp.int32)
    positions = start + jnp.arange(SEQ, dtype=jnp.int32)[None, :]
    return {
        "x": x,
        "p": p,
        "positions": positions,
        "ffn1_w_in": nrm(ks[3], (DEPTH, D_MODEL, 2 * D_FF), D_MODEL ** -0.5),
        "ffn1_w_out": nrm(ks[4], (DEPTH, D_FF, D_MODEL), BETA * D_FF ** -0.5),
        "ln1_g": gain(ks[5], (DEPTH, D_MODEL)),
        "ln1_b": bias(ks[6], (DEPTH, D_MODEL)),
        "w_mix_in": nrm(ks[7], (DEPTH, D_MODEL, MIX_IN_WIDTH), D_MODEL ** -0.5),
        "hgrn_lb_logits": nrm(ks[8], (DEPTH + 1, HGRN_WIDTH), 0.5),
        "hgrn_norm_g": gain(ks[9], (DEPTH, HGRN_V_WIDTH)),
        "w_proj_attn": nrm(ks[10], (DEPTH, ATT_OUT_WIDTH, D_MODEL), ATT_OUT_WIDTH ** -0.5),
        "w_proj_hgrn": nrm(ks[11], (DEPTH, HGRN_V_WIDTH, D_MODEL), HGRN_V_WIDTH ** -0.5),
        "w_mix_out": nrm(ks[12], (DEPTH, D_MODEL, D_MODEL), BETA * D_MODEL ** -0.5),
        "ln2_g": gain(ks[13], (DEPTH, D_MODEL)),
        "ln2_b": bias(ks[14], (DEPTH, D_MODEL)),
        "ffn2_w_in": nrm(ks[15], (DEPTH, D_MODEL, 2 * D_FF), D_MODEL ** -0.5),
        "ffn2_w_out": nrm(ks[16], (DEPTH, D_FF, D_MODEL), BETA * D_FF ** -0.5),
        "ln3_g": gain(ks[17], (DEPTH, D_MODEL)),
        "ln3_b": bias(ks[18], (DEPTH, D_MODEL)),
        "ple_w_proj": nrm(ks[19], (DEPTH, PLE_DIM, D_MODEL), BETA * PLE_DIM ** -0.5),
        "ple_w_gate": nrm(ks[20], (DEPTH, D_MODEL, D_MODEL), D_MODEL ** -0.5),
    }


def reference(x, p, positions, ffn1_w_in, ffn1_w_out, ln1_g, ln1_b, w_mix_in, hgrn_lb_logits,
              hgrn_norm_g, w_proj_attn, w_proj_hgrn, w_mix_out, ln2_g, ln2_b, ffn2_w_in, ffn2_w_out,
              ln3_g, ln3_b, ple_w_proj, ple_w_gate):
    lower_bounds = jnp.cumsum(jax.nn.softmax(hgrn_lb_logits.astype(jnp.float32), axis=0), axis=0)
    h = x
    for i in range(DEPTH):
        h = _layer_norm(ALPHA * h + 0.5 * _swiglu(h, ffn1_w_in[i], ffn1_w_out[i]), ln1_g[i], ln1_b[i])
        mix = _token_mix(h, positions, w_mix_in[i], lower_bounds[i], hgrn_norm_g[i],
                         w_proj_attn[i], w_proj_hgrn[i], w_mix_out[i])
        h = _layer_norm(ALPHA * h + mix, ln2_g[i], ln2_b[i])
        h = _layer_norm(ALPHA * h + 0.5 * _swiglu(h, ffn2_w_in[i], ffn2_w_out[i]), ln3_g[i], ln3_b[i])
        h = h + jax.nn.sigmoid(h @ ple_w_gate[i]) * (p[i] @ ple_w_proj[i])
    return h
```

```python
import functools
import math

import jax
import jax.numpy as jnp
from jax import lax
from jax.experimental import pallas as pl
from jax.experimental.pallas import tpu as pltpu

F32 = jnp.float32
BF16 = jnp.bfloat16

ATT_HEAD_DIM = 64
ATT_HEADS_PER_GROUP = 8
ATT_GROUPS = ((128, 1), (512, 4), (2048, 16))
ATT_GROUP_WIDTH = ATT_HEADS_PER_GROUP * ATT_HEAD_DIM
ATT_WIDTH = ATT_GROUP_WIDTH * len(ATT_GROUPS)
ATT_BLOCK = 128
ROPE_THETA = 500000.0
ROPE_DIM = ATT_HEAD_DIM // 4
HGRN_HEADS = 8
HGRN_DIM = 128
LN_EPS = 1e-5
RMS_EPS = 1e-6

LANES = 128
SUBLANES = 8
VMEM_LIMIT_BYTES = 56 * 1024 * 1024

TOKEN_TILE = 512
FF_CHUNK = 256
PROJ_CHUNK = 512
HGRN_CHUNK = 64
HGRN_STEP_TOKENS = 256
HGRN_BAND = SUBLANES


def _params(*sem):
    return pltpu.CompilerParams(dimension_semantics=sem, vmem_limit_bytes=VMEM_LIMIT_BYTES)


def _resident(shape):
    nd = len(shape)
    return pl.BlockSpec(shape, lambda *_: (0,) * nd)


def _layer_norm(y, g, b):
    mu = jnp.mean(y, axis=-1, keepdims=True)
    yc = y - mu
    var = jnp.mean(yc * yc, axis=-1, keepdims=True)
    return yc * lax.rsqrt(var + LN_EPS) * g + b


def _silu(a):
    return a * jax.nn.sigmoid(a)


def _dot(a, b):
    return jnp.dot(a, b, preferred_element_type=F32)


def _dot_nt(a, b):
    return lax.dot_general(a, b, (((1,), (1,)), ((), ())), preferred_element_type=F32)


def _dot_tn(a, b):
    return lax.dot_general(a, b, (((0,), (0,)), ((), ())), preferred_element_type=F32)


def _ffn_core(x, win_ref, wout_ref, d_ff):
    xb = x.astype(BF16)
    acc = jnp.zeros(x.shape, F32)
    for c in range(d_ff // FF_CHUNK):
        lo = c * FF_CHUNK
        a = _dot(xb, win_ref[:, lo:lo + FF_CHUNK])
        u = _dot(xb, win_ref[:, d_ff + lo:d_ff + lo + FF_CHUNK])
        acc = acc + _dot((_silu(a) * u).astype(BF16), wout_ref[lo:lo + FF_CHUNK, :])
    return acc


def _ffn_ln_kernel(x_ref, win_ref, wout_ref, g_ref, b_ref, o_ref, *, d_ff, alpha):
    x = x_ref[...]
    y = alpha * x + 0.5 * _ffn_core(x, win_ref, wout_ref, d_ff)
    o_ref[...] = _layer_norm(y, g_ref[...], b_ref[...])


def _ffn_ln_ple_kernel(x_ref, p_ref, win_ref, wout_ref, g_ref, b_ref, wpg_ref, wpp_ref, o_ref, *, d_ff, alpha):
    x = x_ref[...]
    y = alpha * x + 0.5 * _ffn_core(x, win_ref, wout_ref, d_ff)
    h = _layer_norm(y, g_ref[...], b_ref[...])
    gate = jax.nn.sigmoid(_dot(h.astype(BF16), wpg_ref[...]))
    emb = _dot(p_ref[...].astype(BF16), wpp_ref[...])
    o_ref[...] = h + gate * emb


def _ffn_ln(x2d, w_in, w_out, g, b, alpha, ple=None):
    t, d = x2d.shape
    d_ff = w_out.shape[0]
    tm = min(TOKEN_TILE, t)
    row = lambda i: (i, 0)
    in_specs = [pl.BlockSpec((tm, d), row)]
    args = [x2d]
    if ple is not None:
        p2d, w_pg, w_pp = ple
        in_specs.append(pl.BlockSpec((tm, p2d.shape[1]), row))
        args.append(p2d)
    in_specs += [_resident(w_in.shape), _resident(w_out.shape), _resident(g.shape), _resident(b.shape)]
    args += [w_in, w_out, g, b]
    if ple is None:
        body = functools.partial(_ffn_ln_kernel, d_ff=d_ff, alpha=alpha)
    else:
        in_specs += [_resident(w_pg.shape), _resident(w_pp.shape)]
        args += [w_pg, w_pp]
        body = functools.partial(_ffn_ln_ple_kernel, d_ff=d_ff, alpha=alpha)
    return pl.pallas_call(
        body,
        grid=(t // tm,),
        in_specs=in_specs,
        out_specs=pl.BlockSpec((tm, d), row),
        out_shape=jax.ShapeDtypeStruct((t, d), F32),
        compiler_params=_params("parallel"),
        name="ffn_ln" if ple is None else "ffn_ln_ple",
    )(*args)


def _proj_kernel(x_ref, w_ref, o_ref):
    xb = x_ref[...].astype(BF16)
    for j in range(w_ref.shape[1] // PROJ_CHUNK):
        sl = slice(j * PROJ_CHUNK, (j + 1) * PROJ_CHUNK)
        o_ref[:, sl] = _dot(xb, w_ref[:, sl]).astype(o_ref.dtype)


def _rope_tables(pos_f32):
    lane = lax.broadcasted_iota(jnp.int32, (1, LANES), 1)
    j = lane & (ATT_HEAD_DIM - 1)
    half = ROPE_DIM // 2
    fidx = (j & (half - 1)).astype(F32)
    inv_freq = jnp.exp(-math.log(ROPE_THETA) * fidx * (2.0 / ROPE_DIM))
    inv_freq = jnp.where(j < ROPE_DIM, inv_freq, 0.0)
    ang = pos_f32 * inv_freq
    cos = jnp.cos(ang)
    sin = jnp.sin(ang)
    sin_lo = jnp.where(j < half, -sin, 0.0)
    sin_hi = jnp.where(j >= half, sin, 0.0)
    return cos, sin_lo, sin_hi


def _proj_rope_kernel(x_ref, pos_ref, w_ref, o_ref, *, q_width, q_scale):
    xb = x_ref[...].astype(BF16)
    cos, sin_lo, sin_hi = _rope_tables(pos_ref[...].astype(F32))
    rep = PROJ_CHUNK // LANES
    cos = jnp.concatenate([cos] * rep, axis=1)
    sin_lo = jnp.concatenate([sin_lo] * rep, axis=1)
    sin_hi = jnp.concatenate([sin_hi] * rep, axis=1)
    half = ROPE_DIM // 2
    for j in range(w_ref.shape[1] // PROJ_CHUNK):
        sl = slice(j * PROJ_CHUNK, (j + 1) * PROJ_CHUNK)
        z = _dot(xb, w_ref[:, sl])
        up = pltpu.roll(z, PROJ_CHUNK - half, 1)
        down = pltpu.roll(z, half, 1)
        z = z * cos + up * sin_lo + down * sin_hi
        if j * PROJ_CHUNK < q_width:
            z = z * q_scale
        o_ref[:, sl] = z.astype(o_ref.dtype)


def _project(x2d, w, out_dtype, pos2d=None, q_width=0):
    t, d = x2d.shape
    n = w.shape[1]
    tm = min(TOKEN_TILE, t)
    row = lambda i: (i, 0)
    if pos2d is None:
        body, in_specs, args, name = _proj_kernel, [pl.BlockSpec((tm, d), row), _resident(w.shape)], (x2d, w), "proj"
    else:
        body = functools.partial(_proj_rope_kernel, q_width=q_width, q_scale=ATT_HEAD_DIM ** -0.5)
        in_specs = [pl.BlockSpec((tm, d), row), pl.BlockSpec((tm, 1), row), _resident(w.shape)]
        args, name = (x2d, pos2d, w), "proj_rope"
    return pl.pallas_call(
        body,
        grid=(t // tm,),
        in_specs=in_specs,
        out_specs=pl.BlockSpec((tm, n), row),
        out_shape=jax.ShapeDtypeStruct((t, n), out_dtype),
        compiler_params=_params("parallel"),
        name=name,
    )(*args)


def _attn_kernel(q_ref, kp_ref, ko_ref, vp_ref, vo_ref, o_ref, lse_ref):
    has_prev = pl.program_id(2) > 0
    blk = ATT_BLOCK
    row = lax.broadcasted_iota(jnp.int32, (blk, blk), 0)
    col = lax.broadcasted_iota(jnp.int32, (blk, blk), 1)
    mask_own = col <= row
    mask_prev = jnp.logical_and(col >= row, has_prev)
    first = lax.broadcasted_iota(jnp.int32, (1, LANES), 1) < ATT_HEAD_DIM
    neg_inf = -jnp.inf
    for p in range(ATT_GROUP_WIDTH // LANES):
        sl = slice(p * LANES, (p + 1) * LANES)
        q2 = q_ref[:, sl]
        kp2, ko2, vp2, vo2 = kp_ref[:, sl], ko_ref[:, sl], vp_ref[:, sl], vo_ref[:, sl]
        zero = jnp.zeros_like(q2)
        outs, lses = [], []
        for qh in (jnp.where(first, q2, zero), jnp.where(first, zero, q2)):
            s_p = jnp.where(mask_prev, _dot_nt(qh, kp2), neg_inf)
            s_o = jnp.where(mask_own, _dot_nt(qh, ko2), neg_inf)
            m = jnp.maximum(jnp.max(s_p, axis=1, keepdims=True), jnp.max(s_o, axis=1, keepdims=True))
            e_p = jnp.exp(s_p - m)
            e_o = jnp.exp(s_o - m)
            l = jnp.sum(e_p, axis=1, keepdims=True) + jnp.sum(e_o, axis=1, keepdims=True)
            o = _dot(e_p.astype(BF16), vp2) + _dot(e_o.astype(BF16), vo2)
            outs.append(o / l)
            lses.append(m + jnp.log(l))
        o_ref[:, sl] = jnp.where(first, outs[0], outs[1])
        lse_ref[:, sl] = jnp.where(first, jnp.broadcast_to(lses[0], (blk, LANES)), jnp.broadcast_to(lses[1], (blk, LANES)))


def _attention_group(qk, rest, v_col, group, dilation, batch, seq):
    w1, w2 = qk.shape[1], rest.shape[1]
    gw = ATT_GROUP_WIDTH
    sub = seq // dilation
    nb = sub // ATT_BLOCK
    qk_v = qk.reshape(batch, sub, dilation * w1)
    rest_v = rest.reshape(batch, sub, dilation * w2)
    qc, kc, vc = group, ATT_WIDTH // gw + group, v_col // gw + group
    n1, n2 = w1 // gw, w2 // gw
    blk = (None, ATT_BLOCK, gw)
    own = lambda c, nn: (lambda b, r, n: (b, n, r * nn + c))
    prev = lambda c, nn: (lambda b, r, n: (b, jnp.maximum(n - 1, 0), r * nn + c))
    out_map = lambda b, r, n: (b, n, r)
    o, lse = pl.pallas_call(
        _attn_kernel,
        grid=(batch, dilation, nb),
        in_specs=[
            pl.BlockSpec(blk, own(qc, n1)),
            pl.BlockSpec(blk, prev(kc, n1)),
            pl.BlockSpec(blk, own(kc, n1)),
            pl.BlockSpec(blk, prev(vc, n2)),
            pl.BlockSpec(blk, own(vc, n2)),
        ],
        out_specs=[pl.BlockSpec(blk, out_map), pl.BlockSpec(blk, out_map)],
        out_shape=[jax.ShapeDtypeStruct((batch, sub, dilation * gw), F32)] * 2,
        compiler_params=_params("parallel", "parallel", "arbitrary"),
        name=f"attn_d{dilation}",
    )(qk_v, qk_v, qk_v, rest_v, rest_v)
    return o.reshape(batch * seq, gw), lse.reshape(batch * seq, gw)


def _split3(x):
    hi = x.astype(BF16)
    r = x - hi.astype(F32)
    mid = r.astype(BF16)
    lo = (r - mid.astype(F32)).astype(BF16)
    return hi, mid, lo


def _hgrn_kernel(lbl_ref, ng_ref, qh_ref, fh_ref, ih_ref, gh_ref, o_ref, st_ref, kbuf, bbuf, vbuf, *, layer, n_chunks):
    c = HGRN_CHUNK
    width = HGRN_HEADS * HGRN_DIM
    band = HGRN_BAND

    @pl.when(pl.program_id(1) == 0)
    def _():
        st_ref[...] = jnp.zeros_like(st_ref)

    pad = jnp.zeros((band, width), F32)
    kbuf[0:band, :] = pad
    bbuf[0:band, :] = pad
    vbuf[0:band, :] = pad

    lgt = lbl_ref[...]
    ex = jnp.exp(lgt - jnp.max(lgt, axis=0, keepdims=True))
    sm = ex / jnp.sum(ex, axis=0, keepdims=True)
    lb = jnp.sum(sm[0:layer + 1, :], axis=0, keepdims=True)

    r_i = lax.broadcasted_iota(jnp.int32, (c, c), 0)
    c_i = lax.broadcasted_iota(jnp.int32, (c, c), 1)
    tri = (c_i <= r_i).astype(BF16)
    rowid = lax.broadcasted_iota(jnp.int32, (c, 1), 0)
    levels = []
    hs = band
    while 2 * hs <= c:
        second = (rowid & (2 * hs - 1)) >= hs
        pair = jnp.logical_and((r_i // (2 * hs)) == (c_i // (2 * hs)),
                               jnp.logical_and((r_i & (2 * hs - 1)) >= hs, (c_i & (2 * hs - 1)) < hs))
        levels.append((hs, second, pair))
        hs *= 2
    in_band = [(rowid & (band - 1)) >= d for d in range(band)]
    ng = ng_ref[...]

    def chunk_body(ci, carry):
        r0 = pl.multiple_of(ci * c, c)
        rows = pl.ds(r0, c)
        fh = fh_ref[rows, :]
        f = lb + (1.0 - lb) * jax.nn.sigmoid(fh)
        lg = jnp.log(f)
        k = 1.0 - f
        q = _silu(qh_ref[rows, :].astype(F32))
        v = ih_ref[rows, :].astype(F32)
        hi, mid, lo = _split3(lg)
        b = _dot(tri, hi) + _dot(tri, mid) + _dot(tri, lo)
        b_last = b[c - 1:c, :]
        qs = (q * jnp.exp(b)).astype(BF16)
        ks = (k * jnp.exp(b_last - b)).astype(BF16)
        decay = jnp.exp(b_last)
        vb = v.astype(BF16)
        lvl_q, lvl_k = [], []
        for hs, second, _ in levels:
            parts = []
            for j in range(c // (2 * hs)):
                e = j * 2 * hs + hs - 1
                parts.append(jnp.broadcast_to(b[e:e + 1, :], (2 * hs, width)))
            bref = jnp.concatenate(parts, axis=0) if len(parts) > 1 else parts[0]
            x = jnp.exp(-jnp.abs(b - bref))
            lvl_q.append(jnp.where(second, q * x, 0.0).astype(BF16))
            lvl_k.append(jnp.where(second, 0.0, k * x).astype(BF16))
        kbuf[band:band + c, :] = k
        bbuf[band:band + c, :] = b
        vbuf[band:band + c, :] = v
        for h in range(HGRN_HEADS):
            sl = slice(h * HGRN_DIM, (h + 1) * HGRN_DIM)
            st = st_ref[h]
            o = _dot_nt(qs[:, sl], st.astype(BF16))
            att = jnp.zeros((c, c), F32)
            for (hs, _, pair), ql, kl in zip(levels, lvl_q, lvl_k):
                att = att + jnp.where(pair, _dot_nt(ql[:, sl], kl[:, sl]), 0.0)
            o = o + _dot(att.astype(BF16), vb[:, sl])
            qh_, bh_ = q[:, sl], b[:, sl]
            for d in range(band):
                shifted = pl.ds(band - d, c)
                kd, bd, vd = kbuf[shifted, sl], bbuf[shifted, sl], vbuf[shifted, sl]
                pr = qh_ * kd * jnp.exp(jnp.minimum(bh_ - bd, 0.0))
                a = jnp.where(in_band[d], jnp.sum(pr, axis=1, keepdims=True), 0.0)
                o = o + a * vd
            st_ref[h] = st * decay[:, sl] + _dot_tn(vb[:, sl], ks[:, sl])
            o = o * lax.rsqrt(jnp.mean(o * o, axis=1, keepdims=True) + RMS_EPS)
            o = o * ng[:, sl] * _silu(gh_ref[rows, sl].astype(F32))
            o_ref[rows, sl] = o.astype(o_ref.dtype)
        return carry

    lax.fori_loop(0, n_chunks, chunk_body, 0)


def _hgrn_scan(rest, fh, lb_logits, norm_g, layer, cols, batch, seq):
    width = HGRN_HEADS * HGRN_DIM
    ts = min(HGRN_STEP_TOKENS, seq)
    steps = seq // ts
    qc, ic, gc = (cidx // width for cidx in cols)
    tok = lambda col: (lambda b, s: (b * steps + s, col))
    body = functools.partial(_hgrn_kernel, layer=layer, n_chunks=ts // HGRN_CHUNK)
    scratch_rows = HGRN_CHUNK + HGRN_BAND
    return pl.pallas_call(
        body,
        grid=(batch, steps),
        in_specs=[
            _resident(lb_logits.shape),
            _resident(norm_g.shape),
            pl.BlockSpec((ts, width), tok(qc)),
            pl.BlockSpec((ts, width), tok(0)),
            pl.BlockSpec((ts, width), tok(ic)),
            pl.BlockSpec((ts, width), tok(gc)),
        ],
        out_specs=pl.BlockSpec((ts, width), tok(0)),
        out_shape=jax.ShapeDtypeStruct((batch * seq, width), BF16),
        scratch_shapes=[
            pltpu.VMEM((HGRN_HEADS, HGRN_DIM, HGRN_DIM), F32),
            pltpu.VMEM((scratch_rows, width), F32),
            pltpu.VMEM((scratch_rows, width), F32),
            pltpu.VMEM((scratch_rows, width), F32),
        ],
        compiler_params=_params("parallel", "arbitrary"),
        name="hgrn_scan",
    )(lb_logits, norm_g, rest, fh, rest, rest)


def _merge_kernel(h_ref, o1_ref, o2_ref, o3_ref, l1_ref, l2_ref, l3_ref, oh_ref, ga_ref, gb_ref,
                  wpa_ref, wpb_ref, wo_ref, g_ref, b_ref, out_ref, *, alpha):
    l1, l2, l3 = l1_ref[...], l2_ref[...], l3_ref[...]
    m = jnp.maximum(jnp.maximum(l1, l2), l3)
    e1, e2, e3 = jnp.exp(l1 - m), jnp.exp(l2 - m), jnp.exp(l3 - m)
    o_att = (e1 * o1_ref[...] + e2 * o2_ref[...] + e3 * o3_ref[...]) / (e1 + e2 + e3)
    y_att = _dot(o_att.astype(BF16), wpa_ref[...])
    y_hgrn = _dot(oh_ref[...], wpb_ref[...])
    merged = (jax.nn.sigmoid(ga_ref[...].astype(F32)) * y_att + jax.nn.sigmoid(gb_ref[...].astype(F32)) * y_hgrn)
    mix = _dot(merged.astype(BF16), wo_ref[...])
    out_ref[...] = _layer_norm(alpha * h_ref[...] + mix, g_ref[...], b_ref[...])


def _merge(h2d, outs, lses, o_hgrn, rest, gate_col, w_pa, w_pb, w_o, g, b, alpha):
    t, d = h2d.shape
    tm = min(TOKEN_TILE, t)
    gw = ATT_GROUP_WIDTH
    row = lambda i: (i, 0)
    gcol = gate_col // d
    in_specs = ([pl.BlockSpec((tm, d), row)] + [pl.BlockSpec((tm, gw), row)] * 6
                + [pl.BlockSpec((tm, d), row),
                   pl.BlockSpec((tm, d), lambda i: (i, gcol)),
                   pl.BlockSpec((tm, d), lambda i: (i, gcol + 1))]
                + [_resident(a.shape) for a in (w_pa, w_pb, w_o, g, b)])
    return pl.pallas_call(
        functools.partial(_merge_kernel, alpha=alpha),
        grid=(t // tm,),
        in_specs=in_specs,
        out_specs=pl.BlockSpec((tm, d), row),
        out_shape=jax.ShapeDtypeStruct((t, d), F32),
        compiler_params=_params("parallel"),
        name="merge_ln",
    )(h2d, *outs, *lses, o_hgrn, rest, rest, w_pa, w_pb, w_o, g, b)


def kernel(x, p, positions, ffn1_w_in, ffn1_w_out, ln1_g, ln1_b, w_mix_in, hgrn_lb_logits, hgrn_norm_g, w_proj_attn, w_proj_hgrn, w_mix_out, ln2_g, ln2_b, ffn2_w_in, ffn2_w_out, ln3_g, ln3_b, ple_w_proj, ple_w_gate):
    batch, seq, d_model = x.shape
    depth = ffn1_w_in.shape[0]
    alpha = (2 * depth) ** 0.25
    width = HGRN_HEADS * HGRN_DIM
    t = batch * seq
    h = x.reshape(t, d_model)
    pos2d = positions.reshape(t, 1)
    lb_logits = hgrn_lb_logits.astype(F32)
    vec = lambda a: a.reshape(1, -1).astype(F32)
    c_v = 2 * ATT_WIDTH
    c_q = 3 * ATT_WIDTH
    c_f, c_i, c_g, c_gate = c_q + width, c_q + 2 * width, c_q + 3 * width, c_q + 4 * width
    for i in range(depth):
        w_in = w_mix_in[i]
        w_qk = w_in[:, :c_v].astype(BF16)
        w_rest = jnp.concatenate([w_in[:, c_q:c_f], w_in[:, c_i:c_g], w_in[:, c_g:c_gate], w_in[:, c_gate:],
                                  w_in[:, c_v:c_q]], axis=1).astype(BF16)
        w_f = w_in[:, c_f:c_i].astype(BF16)
        rest_cols = (0, width, 2 * width)
        rest_gate, rest_v = 3 * width, 3 * width + 2 * d_model

        h = _ffn_ln(h, ffn1_w_in[i].astype(BF16), ffn1_w_out[i].astype(BF16), vec(ln1_g[i]), vec(ln1_b[i]), alpha)
        qk = _project(h, w_qk, BF16, pos2d=pos2d, q_width=ATT_WIDTH)
        rest = _project(h, w_rest, BF16)
        fh = _project(h, w_f, F32)
        outs, lses = [], []
        for gi, (window, dilation) in enumerate(ATT_GROUPS):
            assert window == ATT_BLOCK * dilation and seq % (ATT_BLOCK * dilation) == 0
            o_g, lse_g = _attention_group(qk, rest, rest_v, gi, dilation, batch, seq)
            outs.append(o_g)
            lses.append(lse_g)
        o_hgrn = _hgrn_scan(rest, fh, lb_logits, vec(hgrn_norm_g[i]), i, rest_cols, batch, seq)
        h = _merge(h, outs, lses, o_hgrn, rest, rest_gate, w_proj_attn[i].astype(BF16), w_proj_hgrn[i].astype(BF16),
                   w_mix_out[i].astype(BF16), vec(ln2_g[i]), vec(ln2_b[i]), alpha)
        h = _ffn_ln(h, ffn2_w_in[i].astype(BF16), ffn2_w_out[i].astype(BF16), vec(ln3_g[i]), vec(ln3_b[i]), alpha,
                    ple=(p[i].reshape(t, -1), ple_w_gate[i].astype(BF16), ple_w_proj[i].astype(BF16)))
    return h.reshape(batch, seq, d_model)
```

```python
import functools
import math

import jax
import jax.numpy as jnp
from jax import lax
from jax.experimental import pallas as pl
from jax.experimental.pallas import tpu as pltpu

F32 = jnp.float32
BF16 = jnp.bfloat16

ATT_HEAD_DIM = 64
ATT_HEADS_PER_GROUP = 8
ATT_GROUPS = ((128, 1), (512, 4), (2048, 16))
ATT_GROUP_WIDTH = ATT_HEADS_PER_GROUP * ATT_HEAD_DIM
ATT_WIDTH = ATT_GROUP_WIDTH * len(ATT_GROUPS)
ATT_BLOCK = 128
ROPE_THETA = 500000.0
ROPE_DIM = ATT_HEAD_DIM // 4
HGRN_HEADS = 8
HGRN_DIM = 128
LN_EPS = 1e-5
RMS_EPS = 1e-6

LANES = 128
SUBLANES = 8
VMEM_LIMIT_BYTES = 56 * 1024 * 1024

TOKEN_TILE = 512
FF_CHUNK = 256
PROJ_CHUNK = 512
ATT_STEP_BLOCKS = 4
ATT_STEP_ROWS = ATT_STEP_BLOCKS * ATT_BLOCK
ATT_TILE = ATT_BLOCK * max(d for _, d in ATT_GROUPS)
ROPE_TABLE_ROWS = 256
HGRN_CHUNK = 64
HGRN_STEP_TOKENS = 256
HGRN_BAND = SUBLANES


def _params(*sem):
    return pltpu.CompilerParams(dimension_semantics=sem, vmem_limit_bytes=VMEM_LIMIT_BYTES)


def _resident(shape):
    nd = len(shape)
    return pl.BlockSpec(shape, lambda *_: (0,) * nd)


def _layer_norm(y, g, b):
    mu = jnp.mean(y, axis=-1, keepdims=True)
    yc = y - mu
    var = jnp.mean(yc * yc, axis=-1, keepdims=True)
    return yc * lax.rsqrt(var + LN_EPS) * g + b


def _silu(a):
    return a * jax.nn.sigmoid(a)


def _slab_spec(rows, d, index_map, **kw):
    return pl.BlockSpec((d // LANES, rows, LANES), lambda *i: (0, index_map(*i), 0), **kw)


def _load_slabs(ref, rows=slice(None)):
    return jnp.concatenate([ref[c, rows, :] for c in range(ref.shape[0])], axis=1)


def _store_slabs(ref, val, rows=slice(None)):
    for c in range(ref.shape[0]):
        ref[c, rows, :] = val[:, c * LANES:(c + 1) * LANES]


def _dot(a, b):
    return jnp.dot(a, b, preferred_element_type=F32)


def _dot_nt(a, b):
    return lax.dot_general(a, b, (((1,), (1,)), ((), ())), preferred_element_type=F32)


def _dot_tn(a, b):
    return lax.dot_general(a, b, (((0,), (0,)), ((), ())), preferred_element_type=F32)


def _ffn_core(x, win_ref, wout_ref, d_ff):
    xb = x.astype(BF16)
    acc = jnp.zeros(x.shape, F32)
    for c in range(d_ff // FF_CHUNK):
        lo = c * FF_CHUNK
        a = _dot(xb, win_ref[:, lo:lo + FF_CHUNK])
        u = _dot(xb, win_ref[:, d_ff + lo:d_ff + lo + FF_CHUNK])
        acc = acc + _dot((_silu(a) * u).astype(BF16), wout_ref[lo:lo + FF_CHUNK, :])
    return acc


def _ffn_ln_kernel(x_ref, win_ref, wout_ref, g_ref, b_ref, o_ref, *, d_ff, alpha):
    x = x_ref[...]
    y = alpha * x + 0.5 * _ffn_core(x, win_ref, wout_ref, d_ff)
    _store_slabs(o_ref, _layer_norm(y, g_ref[...], b_ref[...]))


def _ffn_ln_ple_kernel(x_ref, p_ref, win_ref, wout_ref, g_ref, b_ref, wpg_ref, wpp_ref, o_ref, *, d_ff, alpha):
    x = x_ref[...]
    y = alpha * x + 0.5 * _ffn_core(x, win_ref, wout_ref, d_ff)
    h = _layer_norm(y, g_ref[...], b_ref[...])
    gate = jax.nn.sigmoid(_dot(h.astype(BF16), wpg_ref[...]))
    emb = _dot(p_ref[...].astype(BF16), wpp_ref[...])
    o_ref[...] = h + gate * emb


def _ffn_ln(x2d, w_in, w_out, g, b, alpha, ple=None):
    t, d = x2d.shape
    d_ff = w_out.shape[0]
    tm = min(TOKEN_TILE, t)
    row = lambda i: (i, 0)
    in_specs = [pl.BlockSpec((tm, d), row)]
    args = [x2d]
    if ple is not None:
        p2d, w_pg, w_pp = ple
        in_specs.append(pl.BlockSpec((tm, p2d.shape[1]), row))
        args.append(p2d)
    in_specs += [_resident(w_in.shape), _resident(w_out.shape), _resident(g.shape), _resident(b.shape)]
    args += [w_in, w_out, g, b]
    if ple is None:
        body = functools.partial(_ffn_ln_kernel, d_ff=d_ff, alpha=alpha)
    else:
        in_specs += [_resident(w_pg.shape), _resident(w_pp.shape)]
        args += [w_pg, w_pp]
        body = functools.partial(_ffn_ln_ple_kernel, d_ff=d_ff, alpha=alpha)
    return pl.pallas_call(
        body,
        grid=(t // tm,),
        in_specs=in_specs,
        out_specs=_slab_spec(tm, d, lambda i: i) if ple is None else pl.BlockSpec((tm, d), row),
        out_shape=jax.ShapeDtypeStruct((d // LANES, t, LANES) if ple is None else (t, d), F32),
        compiler_params=_params("parallel"),
        name="ffn_ln" if ple is None else "ffn_ln_ple",
    )(*args)


def _proj_kernel(x_ref, w_ref, o_ref):
    xb = _load_slabs(x_ref).astype(BF16)
    for j in range(w_ref.shape[1] // PROJ_CHUNK):
        sl = slice(j * PROJ_CHUNK, (j + 1) * PROJ_CHUNK)
        o_ref[:, sl] = _dot(xb, w_ref[:, sl]).astype(o_ref.dtype)


def _project(x_slabs, w, out_dtype):
    t = x_slabs.shape[1]
    d, n = w.shape
    tm = min(TOKEN_TILE, t)
    return pl.pallas_call(
        _proj_kernel,
        grid=(t // tm,),
        in_specs=[_slab_spec(tm, d, lambda i: i), _resident(w.shape)],
        out_specs=pl.BlockSpec((tm, n), lambda i: (i, 0)),
        out_shape=jax.ShapeDtypeStruct((t, n), out_dtype),
        compiler_params=_params("parallel"),
        name="proj",
    )(x_slabs, w)


def _rope_tables(pos_f32):
    lane = lax.broadcasted_iota(jnp.int32, (1, LANES), 1)
    j = lane & (ATT_HEAD_DIM - 1)
    half = ROPE_DIM // 2
    fidx = (j & (half - 1)).astype(F32)
    inv_freq = jnp.exp(-math.log(ROPE_THETA) * fidx * (2.0 / ROPE_DIM))
    inv_freq = jnp.where(j < ROPE_DIM, inv_freq, 0.0)
    ang = pos_f32 * inv_freq
    cos = jnp.cos(ang)
    sin = jnp.sin(ang)
    sin_lo = jnp.where(j < half, -sin, 0.0)
    sin_hi = jnp.where(j >= half, sin, 0.0)
    return cos, sin_lo, sin_hi


def _merge_lse(o_a, lse_a, o_b, lse_b, want_lse):
    m = jnp.maximum(lse_a, lse_b)
    w_a = jnp.exp(lse_a - m)
    w_b = jnp.exp(lse_b - m)
    den = w_a + w_b
    o = (w_a * o_a + w_b * o_b) / den
    return o, (m + jnp.log(den) if want_lse else None)


def _attn_fused_kernel(x_ref, pos_ref, w_ref, o_ref, cos_s, sinlo_s, sinhi_s, q_buf, kv_ext,
                       carry_near, carry_mid, carry_far, step_o, step_lse, acc_o, acc_lse):
    n = pl.program_id(1)
    blk, gw = ATT_BLOCK, ATT_GROUP_WIDTH
    half = ROPE_DIM // 2
    (_, d_near), (_, d_mid), (_, d_far) = ATT_GROUPS

    @pl.when(n == 0)
    def _():
        carry_near[...] = jnp.zeros_like(carry_near)
        carry_mid[...] = jnp.zeros_like(carry_mid)
        carry_far[...] = jnp.zeros_like(carry_far)

    def fill_tables(i, carry):
        rows = pl.ds(pl.multiple_of(i * ROPE_TABLE_ROWS, ROPE_TABLE_ROWS), ROPE_TABLE_ROWS)
        cos, sin_lo, sin_hi = _rope_tables(pos_ref[rows, :].astype(F32))
        cos_s[rows, :] = cos
        sinlo_s[rows, :] = sin_lo
        sinhi_s[rows, :] = sin_hi
        return carry

    lax.fori_loop(0, ATT_TILE // ROPE_TABLE_ROWS, fill_tables, 0)

    row = lax.broadcasted_iota(jnp.int32, (blk, blk), 0)
    col = lax.broadcasted_iota(jnp.int32, (blk, blk), 1)
    neg_inf = jnp.float32(-jnp.inf)
    bias_own = jnp.where(col <= row, 0.0, neg_inf)
    bias_prev = jnp.where(col >= row, 0.0, neg_inf)
    first = lax.broadcasted_iota(jnp.int32, (1, LANES), 1) < ATT_HEAD_DIM
    ones_a = jnp.broadcast_to(jnp.where(first, 1.0, 0.0).astype(BF16), (blk, LANES))
    ones_b = jnp.broadcast_to(jnp.where(first, 0.0, 1.0).astype(BF16), (blk, LANES))
    rep = gw // LANES

    def project(group, gather):
        xb = jnp.concatenate([gather(x_ref.at[c]) for c in range(x_ref.shape[0])], axis=1).astype(BF16)
        cos = jnp.concatenate([gather(cos_s)] * rep, axis=1)
        sin_lo = jnp.concatenate([gather(sinlo_s)] * rep, axis=1)
        sin_hi = jnp.concatenate([gather(sinhi_s)] * rep, axis=1)
        for part in range(3):
            z = _dot(xb, w_ref[group, :, part * gw:(part + 1) * gw])
            if part < 2:
                up = pltpu.roll(z, gw - half, 1)
                down = pltpu.roll(z, half, 1)
                z = z * cos + up * sin_lo + down * sin_hi
            if part == 0:
                q_buf[...] = (z * ATT_HEAD_DIM ** -0.5).astype(BF16)
            else:
                kv_ext[blk:blk + ATT_STEP_ROWS, (part - 1) * gw:part * gw] = z.astype(BF16)

    def run_blocks(prev_ref, prev_base, prev_valid):
        def body(a, carry):
            off = pl.multiple_of(a * blk, blk)
            q = q_buf[pl.ds(off, blk), :]
            own = kv_ext[pl.ds(off + blk, blk), :]
            prev = prev_ref[pl.ds(prev_base + off, blk), :]
            bias = jnp.concatenate([jnp.where(prev_valid(a), bias_prev, neg_inf), bias_own], axis=1)
            for p in range(rep):
                sl = slice(p * LANES, (p + 1) * LANES)
                vsl = slice(gw + p * LANES, gw + (p + 1) * LANES)
                q2 = q[:, sl]
                zero = jnp.zeros_like(q2)
                q_cat = jnp.concatenate([jnp.where(first, q2, zero), jnp.where(first, zero, q2)], axis=0)
                k_cat = jnp.concatenate([prev[:, sl], own[:, sl]], axis=0)
                s = _dot_nt(q_cat, k_cat)
                e_parts, m_parts = [], []
                for hd in range(2):
                    s_h = s[hd * blk:(hd + 1) * blk, :] + bias
                    m = jnp.max(s_h, axis=1, keepdims=True)
                    e_parts.append(jnp.exp(s_h - m).astype(BF16))
                    m_parts.append(m)
                vp2, vo2 = prev[:, vsl], own[:, vsl]
                v_cat = jnp.concatenate([
                    jnp.concatenate([jnp.where(first, vp2, zero), ones_a], axis=1),
                    jnp.concatenate([jnp.where(first, vo2, zero), ones_a], axis=1),
                    jnp.concatenate([jnp.where(first, zero, vp2), ones_b], axis=1),
                    jnp.concatenate([jnp.where(first, zero, vo2), ones_b], axis=1)], axis=0)
                r = _dot(jnp.concatenate(e_parts, axis=1), v_cat)
                den = r[:, LANES:]
                step_o[pl.ds(off, blk), sl] = r[:, :LANES] / den
                step_lse[pl.ds(off, blk), sl] = jnp.where(first, m_parts[0], m_parts[1]) + jnp.log(den)
            return carry

        lax.fori_loop(0, ATT_STEP_BLOCKS, body, 0)

    seen_tile = n > 0
    steps = ATT_TILE // ATT_STEP_ROWS

    kv_ext[0:blk, :] = carry_near[...]
    for j in range(steps):
        rows = slice(j * ATT_STEP_ROWS, (j + 1) * ATT_STEP_ROWS)
        project(0, lambda ref, rows=rows: ref[rows, :])
        run_blocks(kv_ext, 0, (lambda a: True) if j > 0 else (lambda a: jnp.logical_or(a > 0, seen_tile)))
        kv_ext[0:blk, :] = kv_ext[ATT_STEP_ROWS:ATT_STEP_ROWS + blk, :]
        _store_slabs(acc_o, step_o[...], rows)
        _store_slabs(acc_lse, step_lse[...], rows)
    carry_near[...] = kv_ext[0:blk, :]

    assert ATT_TILE // d_mid == ATT_STEP_ROWS
    for r in range(d_mid):
        rows = pl.ds(r, ATT_STEP_ROWS, stride=d_mid)
        kv_ext[0:blk, :] = carry_mid[r * blk:(r + 1) * blk, :]
        project(1, lambda ref, rows=rows: ref[rows, :])
        run_blocks(kv_ext, 0, lambda a: jnp.logical_or(a > 0, seen_tile))
        carry_mid[r * blk:(r + 1) * blk, :] = kv_ext[ATT_STEP_ROWS:ATT_STEP_ROWS + blk, :]
        o, lse = _merge_lse(_load_slabs(acc_o, rows), _load_slabs(acc_lse, rows), step_o[...], step_lse[...], True)
        _store_slabs(acc_o, o, rows)
        _store_slabs(acc_lse, lse, rows)

    assert ATT_TILE // d_far == blk
    for j in range(d_far // ATT_STEP_BLOCKS):
        res = [j * ATT_STEP_BLOCKS + a for a in range(ATT_STEP_BLOCKS)]
        project(2, lambda ref, res=res: jnp.concatenate([ref[pl.ds(r, blk, stride=d_far), :] for r in res], axis=0))
        run_blocks(carry_far, j * ATT_STEP_ROWS, lambda a: seen_tile)
        carry_far[j * ATT_STEP_ROWS:(j + 1) * ATT_STEP_ROWS, :] = kv_ext[blk:blk + ATT_STEP_ROWS, :]
        for a, r in enumerate(res):
            rows = pl.ds(r, blk, stride=d_far)
            o, _ = _merge_lse(_load_slabs(acc_o, rows), _load_slabs(acc_lse, rows), step_o[a * blk:(a + 1) * blk, :],
                              step_lse[a * blk:(a + 1) * blk, :], False)
            _store_slabs(acc_o, o, rows)

    o_ref[...] = _load_slabs(acc_o).astype(o_ref.dtype)


def _attention(h_slabs, pos2d, w_qkv, batch, seq):
    t, d = h_slabs.shape[1], w_qkv.shape[1]
    gw = ATT_GROUP_WIDTH
    assert seq % ATT_TILE == 0
    tiles = seq // ATT_TILE
    tok = lambda b, n: (b * tiles + n, 0)
    kv_rows = ATT_BLOCK + ATT_STEP_ROWS
    d_mid, d_far = ATT_GROUPS[1][1], ATT_GROUPS[2][1]
    return pl.pallas_call(
        _attn_fused_kernel,
        grid=(batch, tiles),
        in_specs=[
            _slab_spec(ATT_TILE, d, lambda b, n: b * tiles + n, pipeline_mode=pl.Buffered(1)),
            pl.BlockSpec((ATT_TILE, 1), tok),
            _resident(w_qkv.shape),
        ],
        out_specs=pl.BlockSpec((ATT_TILE, gw), tok),
        out_shape=jax.ShapeDtypeStruct((t, gw), BF16),
        scratch_shapes=[
            pltpu.VMEM((ATT_TILE, LANES), F32),
            pltpu.VMEM((ATT_TILE, LANES), F32),
            pltpu.VMEM((ATT_TILE, LANES), F32),
            pltpu.VMEM((ATT_STEP_ROWS, gw), BF16),
            pltpu.VMEM((kv_rows, 2 * gw), BF16),
            pltpu.VMEM((ATT_BLOCK, 2 * gw), BF16),
            pltpu.VMEM((d_mid * ATT_BLOCK, 2 * gw), BF16),
            pltpu.VMEM((d_far * ATT_BLOCK, 2 * gw), BF16),
            pltpu.VMEM((ATT_STEP_ROWS, gw), F32),
            pltpu.VMEM((ATT_STEP_ROWS, gw), F32),
            pltpu.VMEM((gw // LANES, ATT_TILE, LANES), F32),
            pltpu.VMEM((gw // LANES, ATT_TILE, LANES), F32),
        ],
        compiler_params=_params("parallel", "arbitrary"),
        name="attn_fused",
    )(h_slabs, pos2d, w_qkv)


def _split3(x):
    hi = x.astype(BF16)
    r = x - hi.astype(F32)
    mid = r.astype(BF16)
    lo = (r - mid.astype(F32)).astype(BF16)
    return hi, mid, lo


def _hgrn_kernel(lbl_ref, ng_ref, qh_ref, fh_ref, ih_ref, gh_ref, o_ref, st_ref, kbuf, bbuf, vbuf, *, layer, n_chunks):
    c = HGRN_CHUNK
    width = HGRN_HEADS * HGRN_DIM
    band = HGRN_BAND

    @pl.when(pl.program_id(1) == 0)
    def _():
        st_ref[...] = jnp.zeros_like(st_ref)

    pad = jnp.zeros((band, width), F32)
    kbuf[0:band, :] = pad
    bbuf[0:band, :] = pad
    vbuf[0:band, :] = pad

    lgt = lbl_ref[...]
    ex = jnp.exp(lgt - jnp.max(lgt, axis=0, keepdims=True))
    sm = ex / jnp.sum(ex, axis=0, keepdims=True)
    lb = jnp.sum(sm[0:layer + 1, :], axis=0, keepdims=True)

    r_i = lax.broadcasted_iota(jnp.int32, (c, c), 0)
    c_i = lax.broadcasted_iota(jnp.int32, (c, c), 1)
    tri = (c_i <= r_i).astype(BF16)
    rowid = lax.broadcasted_iota(jnp.int32, (c, 1), 0)
    levels = []
    hs = band
    while 2 * hs <= c:
        second = (rowid & (2 * hs - 1)) >= hs
        pair = jnp.logical_and((r_i // (2 * hs)) == (c_i // (2 * hs)),
                               jnp.logical_and((r_i & (2 * hs - 1)) >= hs, (c_i & (2 * hs - 1)) < hs))
        levels.append((hs, second, pair))
        hs *= 2
    in_band = [(rowid & (band - 1)) >= d for d in range(band)]
    ng = ng_ref[...]

    def chunk_body(ci, carry):
        r0 = pl.multiple_of(ci * c, c)
        rows = pl.ds(r0, c)
        fh = fh_ref[rows, :]
        f = lb + (1.0 - lb) * jax.nn.sigmoid(fh)
        lg = jnp.log(f)
        k = 1.0 - f
        q = _silu(qh_ref[rows, :].astype(F32))
        v = ih_ref[rows, :].astype(F32)
        hi, mid, lo = _split3(lg)
        b = _dot(tri, hi) + _dot(tri, mid) + _dot(tri, lo)
        b_last = b[c - 1:c, :]
        qs = (q * jnp.exp(b)).astype(BF16)
        ks = (k * jnp.exp(b_last - b)).astype(BF16)
        decay = jnp.exp(b_last)
        vb = v.astype(BF16)
        lvl_q, lvl_k = [], []
        for hs, second, _ in levels:
            parts = []
            for j in range(c // (2 * hs)):
                e = j * 2 * hs + hs - 1
                parts.append(jnp.broadcast_to(b[e:e + 1, :], (2 * hs, width)))
            bref = jnp.concatenate(parts, axis=0) if len(parts) > 1 else parts[0]
            x = jnp.exp(-jnp.abs(b - bref))
            lvl_q.append(jnp.where(second, q * x, 0.0).astype(BF16))
            lvl_k.append(jnp.where(second, 0.0, k * x).astype(BF16))
        kbuf[band:band + c, :] = k
        bbuf[band:band + c, :] = b
        vbuf[band:band + c, :] = v
        for h in range(HGRN_HEADS):
            sl = slice(h * HGRN_DIM, (h + 1) * HGRN_DIM)
            st = st_ref[h]
            o = _dot_nt(qs[:, sl], st.astype(BF16))
            att = jnp.zeros((c, c), F32)
            for (hs, _, pair), ql, kl in zip(levels, lvl_q, lvl_k):
                att = att + jnp.where(pair, _dot_nt(ql[:, sl], kl[:, sl]), 0.0)
            o = o + _dot(att.astype(BF16), vb[:, sl])
            qh_, bh_ = q[:, sl], b[:, sl]
            for d in range(band):
                shifted = pl.ds(band - d, c)
                kd, bd, vd = kbuf[shifted, sl], bbuf[shifted, sl], vbuf[shifted, sl]
                pr = qh_ * kd * jnp.exp(jnp.minimum(bh_ - bd, 0.0))
                a = jnp.where(in_band[d], jnp.sum(pr, axis=1, keepdims=True), 0.0)
                o = o + a * vd
            st_ref[h] = st * decay[:, sl] + _dot_tn(vb[:, sl], ks[:, sl])
            o = o * lax.rsqrt(jnp.mean(o * o, axis=1, keepdims=True) + RMS_EPS)
            o = o * ng[:, sl] * _silu(gh_ref[rows, sl].astype(F32))
            o_ref[rows, sl] = o.astype(o_ref.dtype)
        return carry

    lax.fori_loop(0, n_chunks, chunk_body, 0)


def _hgrn_scan(rest, fh, lb_logits, norm_g, layer, cols, batch, seq):
    width = HGRN_HEADS * HGRN_DIM
    ts = min(HGRN_STEP_TOKENS, seq)
    steps = seq // ts
    qc, ic, gc = (cidx // width for cidx in cols)
    tok = lambda col: (lambda b, s: (b * steps + s, col))
    body = functools.partial(_hgrn_kernel, layer=layer, n_chunks=ts // HGRN_CHUNK)
    scratch_rows = HGRN_CHUNK + HGRN_BAND
    return pl.pallas_call(
        body,
        grid=(batch, steps),
        in_specs=[
            _resident(lb_logits.shape),
            _resident(norm_g.shape),
            pl.BlockSpec((ts, width), tok(qc)),
            pl.BlockSpec((ts, width), tok(0)),
            pl.BlockSpec((ts, width), tok(ic)),
            pl.BlockSpec((ts, width), tok(gc)),
        ],
        out_specs=pl.BlockSpec((ts, width), tok(0)),
        out_shape=jax.ShapeDtypeStruct((batch * seq, width), BF16),
        scratch_shapes=[
            pltpu.VMEM((HGRN_HEADS, HGRN_DIM, HGRN_DIM), F32),
            pltpu.VMEM((scratch_rows, width), F32),
            pltpu.VMEM((scratch_rows, width), F32),
            pltpu.VMEM((scratch_rows, width), F32),
        ],
        compiler_params=_params("parallel", "arbitrary"),
        name="hgrn_scan",
    )(lb_logits, norm_g, rest, fh, rest, rest)


def _merge_kernel(h_ref, oa_ref, oh_ref, ga_ref, gb_ref, wpa_ref, wpb_ref, wo_ref, g_ref, b_ref, out_ref, *, alpha):
    y_att = _dot(oa_ref[...], wpa_ref[...])
    y_hgrn = _dot(oh_ref[...], wpb_ref[...])
    merged = (jax.nn.sigmoid(ga_ref[...].astype(F32)) * y_att + jax.nn.sigmoid(gb_ref[...].astype(F32)) * y_hgrn)
    mix = _dot(merged.astype(BF16), wo_ref[...])
    out_ref[...] = _layer_norm(alpha * _load_slabs(h_ref) + mix, g_ref[...], b_ref[...])


def _merge(h_slabs, o_att, o_hgrn, rest, gate_col, w_pa, w_pb, w_o, g, b, alpha):
    t, d = o_hgrn.shape
    tm = min(TOKEN_TILE, t)
    row = lambda i: (i, 0)
    gcol = gate_col // d
    in_specs = [_slab_spec(tm, d, lambda i: i),
                pl.BlockSpec((tm, o_att.shape[1]), row),
                pl.BlockSpec((tm, d), row),
                pl.BlockSpec((tm, d), lambda i: (i, gcol)),
                pl.BlockSpec((tm, d), lambda i: (i, gcol + 1))]
    in_specs += [_resident(a.shape) for a in (w_pa, w_pb, w_o, g, b)]
    return pl.pallas_call(
        functools.partial(_merge_kernel, alpha=alpha),
        grid=(t // tm,),
        in_specs=in_specs,
        out_specs=pl.BlockSpec((tm, d), row),
        out_shape=jax.ShapeDtypeStruct((t, d), F32),
        compiler_params=_params("parallel"),
        name="merge_ln",
    )(h_slabs, o_att, o_hgrn, rest, rest, w_pa, w_pb, w_o, g, b)


def kernel(x, p, positions, ffn1_w_in, ffn1_w_out, ln1_g, ln1_b, w_mix_in, hgrn_lb_logits, hgrn_norm_g, w_proj_attn, w_proj_hgrn, w_mix_out, ln2_g, ln2_b, ffn2_w_in, ffn2_w_out, ln3_g, ln3_b, ple_w_proj, ple_w_gate):
    batch, seq, d_model = x.shape
    depth = ffn1_w_in.shape[0]
    alpha = (2 * depth) ** 0.25
    width = HGRN_HEADS * HGRN_DIM
    gw = ATT_GROUP_WIDTH
    t = batch * seq
    h = x.reshape(t, d_model)
    pos2d = positions.reshape(t, 1)
    lb_logits = hgrn_lb_logits.astype(F32)
    vec = lambda a: a.reshape(1, -1).astype(F32)
    c_q = 3 * ATT_WIDTH
    c_f, c_i, c_g, c_gate = c_q + width, c_q + 2 * width, c_q + 3 * width, c_q + 4 * width
    for i in range(depth):
        w_in = w_mix_in[i]
        w_qkv = jnp.stack([
            jnp.concatenate([w_in[:, part * ATT_WIDTH + g * gw:part * ATT_WIDTH + (g + 1) * gw] for part in range(3)], axis=1)
            for g in range(len(ATT_GROUPS))]).astype(BF16)
        w_rest = jnp.concatenate([w_in[:, c_q:c_f], w_in[:, c_i:c_g], w_in[:, c_g:c_gate], w_in[:, c_gate:]], axis=1).astype(BF16)
        w_f = w_in[:, c_f:c_i].astype(BF16)
        rest_cols = (0, width, 2 * width)
        rest_gate = 3 * width

        h = _ffn_ln(h, ffn1_w_in[i].astype(BF16), ffn1_w_out[i].astype(BF16), vec(ln1_g[i]), vec(ln1_b[i]), alpha)
        for window, dilation in ATT_GROUPS:
            assert window == ATT_BLOCK * dilation
        o_att = _attention(h, pos2d, w_qkv, batch, seq)
        rest = _project(h, w_rest, BF16)
        fh = _project(h, w_f, F32)
        o_hgrn = _hgrn_scan(rest, fh, lb_logits, vec(hgrn_norm_g[i]), i, rest_cols, batch, seq)
        h = _merge(h, o_att, o_hgrn, rest, rest_gate, w_proj_attn[i].astype(BF16), w_proj_hgrn[i].astype(BF16),
                   w_mix_out[i].astype(BF16), vec(ln2_g[i]), vec(ln2_b[i]), alpha)
        h = _ffn_ln(h, ffn2_w_in[i].astype(BF16), ffn2_w_out[i].astype(BF16), vec(ln3_g[i]), vec(ln3_b[i]), alpha,
                    ple=(p[i].reshape(t, -1), ple_w_gate[i].astype(BF16), ple_w_proj[i].astype(BF16)))
    return h.reshape(batch, seq, d_model)
```

```python
import functools
import math

import jax
import jax.numpy as jnp
from jax import lax
from jax.experimental import pallas as pl
from jax.experimental.pallas import tpu as pltpu

F32 = jnp.float32
BF16 = jnp.bfloat16

ATT_HEAD_DIM = 64
ATT_HEADS_PER_GROUP = 8
ATT_GROUPS = ((128, 1), (512, 4), (2048, 16))
ATT_GROUP_WIDTH = ATT_HEADS_PER_GROUP * ATT_HEAD_DIM
ATT_WIDTH = ATT_GROUP_WIDTH * len(ATT_GROUPS)
ATT_BLOCK = 128
ROPE_THETA = 500000.0
ROPE_DIM = ATT_HEAD_DIM // 4
HGRN_HEADS = 8
HGRN_DIM = 128
LN_EPS = 1e-5
RMS_EPS = 1e-6

LANES = 128
SUBLANES = 8
VMEM_LIMIT_BYTES = 56 * 1024 * 1024

TOKEN_TILE = 512
FF_CHUNK = 256
PROJ_CHUNK = 512
ATT_STEP_BLOCKS = 4
ATT_STEP_ROWS = ATT_STEP_BLOCKS * ATT_BLOCK
ATT_TILE = ATT_BLOCK * max(d for _, d in ATT_GROUPS)
ROPE_TABLE_ROWS = 256
HGRN_CHUNK = 64
HGRN_STEP_TOKENS = 512


def _params(*sem):
    return pltpu.CompilerParams(dimension_semantics=sem, vmem_limit_bytes=VMEM_LIMIT_BYTES)


def _resident(shape):
    nd = len(shape)
    return pl.BlockSpec(shape, lambda *_: (0,) * nd)


def _layer_norm(y, g, b):
    mu = jnp.mean(y, axis=-1, keepdims=True)
    yc = y - mu
    var = jnp.mean(yc * yc, axis=-1, keepdims=True)
    return yc * lax.rsqrt(var + LN_EPS) * g + b


def _silu(a):
    return a * jax.nn.sigmoid(a)


def _slab_spec(rows, d, index_map, **kw):
    return pl.BlockSpec((d // LANES, rows, LANES), lambda *i: (0, index_map(*i), 0), **kw)


def _load_slabs(ref, rows=slice(None)):
    return jnp.concatenate([ref[c, rows, :] for c in range(ref.shape[0])], axis=1)


def _store_slabs(ref, val, rows=slice(None)):
    for c in range(ref.shape[0]):
        ref[c, rows, :] = val[:, c * LANES:(c + 1) * LANES]


def _dot(a, b):
    return jnp.dot(a, b, preferred_element_type=F32)


def _dot_nt(a, b):
    return lax.dot_general(a, b, (((1,), (1,)), ((), ())), preferred_element_type=F32)


def _dot_tn(a, b):
    return lax.dot_general(a, b, (((0,), (0,)), ((), ())), preferred_element_type=F32)


def _ffn_core(x, win_ref, wout_ref, d_ff):
    xb = x.astype(BF16)
    acc = jnp.zeros(x.shape, F32)
    for c in range(d_ff // FF_CHUNK):
        lo = c * FF_CHUNK
        a = _dot(xb, win_ref[:, lo:lo + FF_CHUNK])
        u = _dot(xb, win_ref[:, d_ff + lo:d_ff + lo + FF_CHUNK])
        acc = acc + _dot((_silu(a) * u).astype(BF16), wout_ref[lo:lo + FF_CHUNK, :])
    return acc


def _ffn_ln_kernel(x_ref, win_ref, wout_ref, g_ref, b_ref, o_ref, *, d_ff, alpha):
    x = x_ref[...]
    y = alpha * x + 0.5 * _ffn_core(x, win_ref, wout_ref, d_ff)
    _store_slabs(o_ref, _layer_norm(y, g_ref[...], b_ref[...]))


def _ffn_ln_ple_kernel(x_ref, p_ref, win_ref, wout_ref, g_ref, b_ref, wpg_ref, wpp_ref, o_ref, *, d_ff, alpha):
    x = x_ref[...]
    y = alpha * x + 0.5 * _ffn_core(x, win_ref, wout_ref, d_ff)
    h = _layer_norm(y, g_ref[...], b_ref[...])
    gate = jax.nn.sigmoid(_dot(h.astype(BF16), wpg_ref[...]))
    emb = _dot(p_ref[...].astype(BF16), wpp_ref[...])
    o_ref[...] = h + gate * emb


def _ffn_ln(x2d, w_in, w_out, g, b, alpha, ple=None):
    t, d = x2d.shape
    d_ff = w_out.shape[0]
    tm = min(TOKEN_TILE, t)
    row = lambda i: (i, 0)
    in_specs = [pl.BlockSpec((tm, d), row)]
    args = [x2d]
    if ple is not None:
        p2d, w_pg, w_pp = ple
        in_specs.append(pl.BlockSpec((tm, p2d.shape[1]), row))
        args.append(p2d)
    in_specs += [_resident(w_in.shape), _resident(w_out.shape), _resident(g.shape), _resident(b.shape)]
    args += [w_in, w_out, g, b]
    if ple is None:
        body = functools.partial(_ffn_ln_kernel, d_ff=d_ff, alpha=alpha)
    else:
        in_specs += [_resident(w_pg.shape), _resident(w_pp.shape)]
        args += [w_pg, w_pp]
        body = functools.partial(_ffn_ln_ple_kernel, d_ff=d_ff, alpha=alpha)
    return pl.pallas_call(
        body,
        grid=(t // tm,),
        in_specs=in_specs,
        out_specs=_slab_spec(tm, d, lambda i: i) if ple is None else pl.BlockSpec((tm, d), row),
        out_shape=jax.ShapeDtypeStruct((d // LANES, t, LANES) if ple is None else (t, d), F32),
        compiler_params=_params("parallel"),
        name="ffn_ln" if ple is None else "ffn_ln_ple",
    )(*args)


def _pair_lane_order():
    hd, half = ATT_HEAD_DIM, ROPE_DIM // 2
    a, b = list(range(hd)), list(range(hd, 2 * hd))
    order = (a[:half] + b[:half] + a[ROPE_DIM:] + a[half:ROPE_DIM] + b[half:ROPE_DIM] + b[ROPE_DIM:])
    return order, [src < hd for src in order]


def _rope_tables(pos_f32):
    lane = lax.broadcasted_iota(jnp.int32, (1, LANES), 1)
    half = ROPE_DIM // 2
    j = lane & (ATT_HEAD_DIM - 1)
    fidx = (j & (half - 1)).astype(F32)
    inv_freq = jnp.exp(-math.log(ROPE_THETA) * fidx * (2.0 / ROPE_DIM))
    inv_freq = jnp.where(j < ROPE_DIM, inv_freq, 0.0)
    ang = pos_f32 * inv_freq
    return jnp.cos(ang), jnp.where(lane < ATT_HEAD_DIM, -1.0, 1.0) * jnp.sin(ang)


def _merge_lse(o_a, lse_a, o_b, lse_b, want_lse):
    m = jnp.maximum(lse_a, lse_b)
    w_a = jnp.exp(lse_a - m)
    w_b = jnp.exp(lse_b - m)
    den = w_a + w_b
    o = (w_a * o_a + w_b * o_b) / den
    return o, (m + jnp.log(den) if want_lse else None)


def _attn_fused_kernel(x_ref, pos_ref, w_ref, o_ref, cos_s, sin_s, xg, cos_g, sin_g, q_buf, kv_ext,
                       carry_near, carry_mid, carry_far, step_o, step_lse, acc_o, acc_lse):
    n = pl.program_id(1)
    blk, gw = ATT_BLOCK, ATT_GROUP_WIDTH
    (_, d_near), (_, d_mid), (_, d_far) = ATT_GROUPS
    region = ATT_TILE // d_mid
    assert d_near == 1 and d_far == d_mid * d_mid and region == ATT_STEP_ROWS and d_mid == ATT_STEP_BLOCKS

    @pl.when(n == 0)
    def _():
        carry_near[...] = jnp.zeros_like(carry_near)
        carry_mid[...] = jnp.zeros_like(carry_mid)
        carry_far[...] = jnp.zeros_like(carry_far)

    def fill_tables(i, carry):
        rows = pl.ds(pl.multiple_of(i * ROPE_TABLE_ROWS, ROPE_TABLE_ROWS), ROPE_TABLE_ROWS)
        cos, sin = _rope_tables(pos_ref[rows, :].astype(F32))
        cos_s[rows, :] = cos
        sin_s[rows, :] = sin
        return carry

    lax.fori_loop(0, ATT_TILE // ROPE_TABLE_ROWS, fill_tables, 0)

    row = lax.broadcasted_iota(jnp.int32, (blk, blk), 0)
    col = lax.broadcasted_iota(jnp.int32, (blk, blk), 1)
    neg_inf = jnp.float32(-jnp.inf)
    bias_own = jnp.where(col <= row, 0.0, neg_inf)
    bias_prev = jnp.where(col >= row, 0.0, neg_inf)
    lane = lax.broadcasted_iota(jnp.int32, (1, LANES), 1)
    first = lane < ATT_HEAD_DIM
    lane_j = lane & (ATT_HEAD_DIM - 1)
    half = ROPE_DIM // 2
    first_qk = jnp.logical_or(lane_j < half, jnp.logical_and(lane < ATT_HEAD_DIM, lane_j >= ROPE_DIM))
    ones_a = jnp.broadcast_to(jnp.where(first, 1.0, 0.0).astype(BF16), (blk, LANES))
    ones_b = jnp.broadcast_to(jnp.where(first, 0.0, 1.0).astype(BF16), (blk, LANES))
    rep = gw // LANES

    def project(group, xb, cos, sin):
        for part in range(3):
            z = _dot(xb, w_ref[group, :, part * gw:(part + 1) * gw])
            if part < 2:
                slabs = []
                for c in range(rep):
                    zc = z[:, c * LANES:(c + 1) * LANES]
                    slabs.append(zc * cos + pltpu.roll(zc, ATT_HEAD_DIM, 1) * sin)
                z = jnp.concatenate(slabs, axis=1)
            if part == 0:
                q_buf[...] = (z * ATT_HEAD_DIM ** -0.5).astype(BF16)
            else:
                kv_ext[blk:blk + ATT_STEP_ROWS, (part - 1) * gw:part * gw] = z.astype(BF16)

    def run_blocks(prev_ref, prev_base, prev_valid):
        def scores(a):
            off = pl.multiple_of(a * blk, blk)
            q = q_buf[pl.ds(off, blk), :]
            own = kv_ext[pl.ds(off + blk, blk), 0:gw]
            prev = prev_ref[pl.ds(prev_base + off, blk), 0:gw]
            bias = jnp.concatenate([jnp.where(prev_valid(a), bias_prev, neg_inf), bias_own], axis=1)
            zero = jnp.zeros((blk, LANES), BF16)
            e_all, m_all = [], []
            for p in range(rep):
                sl = slice(p * LANES, (p + 1) * LANES)
                q2 = q[:, sl]
                q_cat = jnp.concatenate([jnp.where(first_qk, q2, zero), jnp.where(first_qk, zero, q2)], axis=0)
                k_cat = jnp.concatenate([prev[:, sl], own[:, sl]], axis=0)
                s = _dot_nt(q_cat, k_cat)
                e_parts, m_parts = [], []
                for hd in range(2):
                    s_h = s[hd * blk:(hd + 1) * blk, :] + bias
                    m = jnp.max(s_h, axis=1, keepdims=True)
                    e_parts.append(jnp.exp(s_h - m).astype(BF16))
                    m_parts.append(m)
                e_all.append(jnp.concatenate(e_parts, axis=1))
                m_all.append(jnp.where(first, m_parts[0], m_parts[1]))
            return e_all, m_all

        def values(a, e_all, m_all):
            off = pl.multiple_of(a * blk, blk)
            own = kv_ext[pl.ds(off + blk, blk), gw:2 * gw]
            prev = prev_ref[pl.ds(prev_base + off, blk), gw:2 * gw]
            zero = jnp.zeros((blk, LANES), BF16)
            for p in range(rep):
                sl = slice(p * LANES, (p + 1) * LANES)
                vp2, vo2 = prev[:, sl], own[:, sl]
                v_cat = jnp.concatenate([
                    jnp.concatenate([jnp.where(first, vp2, zero), ones_a], axis=1),
                    jnp.concatenate([jnp.where(first, vo2, zero), ones_a], axis=1),
                    jnp.concatenate([jnp.where(first, zero, vp2), ones_b], axis=1),
                    jnp.concatenate([jnp.where(first, zero, vo2), ones_b], axis=1)], axis=0)
                r = _dot(e_all[p], v_cat)
                den = r[:, LANES:]
                step_o[p, pl.ds(off, blk), :] = r[:, :LANES] / den
                step_lse[p, pl.ds(off, blk), :] = m_all[p] + jnp.log(den)

        def body(i, carry):
            pair_of_blocks = [2 * i, 2 * i + 1]
            staged = [scores(a) for a in pair_of_blocks]
            for a, (e_all, m_all) in zip(pair_of_blocks, staged):
                values(a, e_all, m_all)
            return carry

        lax.fori_loop(0, ATT_STEP_BLOCKS // 2, body, 0)

    def rows_of(ref3, rows):
        return jnp.concatenate([ref3[c, rows, :] for c in range(ref3.shape[0])], axis=1)

    seen_tile = n > 0
    chained = lambda a: jnp.logical_or(a > 0, seen_tile)

    kv_ext[0:blk, :] = carry_near[...]
    for j in range(ATT_TILE // ATT_STEP_ROWS):
        rows = slice(j * ATT_STEP_ROWS, (j + 1) * ATT_STEP_ROWS)
        project(0, rows_of(x_ref, rows).astype(BF16), cos_s[rows, :], sin_s[rows, :])
        run_blocks(kv_ext, 0, (lambda a: True) if j > 0 else chained)
        kv_ext[0:blk, :] = kv_ext[ATT_STEP_ROWS:ATT_STEP_ROWS + blk, :]
        for rho in range(d_mid):
            src = pl.ds(rho, blk, stride=d_mid)
            dst = slice(rho * region + j * blk, rho * region + (j + 1) * blk)
            for c in range(rep):
                acc_o[c, dst, :] = step_o[c, src, :]
                acc_lse[c, dst, :] = step_lse[c, src, :]
    carry_near[...] = kv_ext[0:blk, :]

    for rho in range(d_mid):
        src = pl.ds(rho, region, stride=d_mid)
        reg = slice(rho * region, (rho + 1) * region)
        for c in range(xg.shape[0]):
            xg[c] = x_ref[c, src, :]
        cos_g[...] = cos_s[src, :]
        sin_g[...] = sin_s[src, :]

        kv_ext[0:blk, :] = carry_mid[rho * blk:(rho + 1) * blk, :]
        project(1, rows_of(xg, slice(None)).astype(BF16), cos_g[...], sin_g[...])
        run_blocks(kv_ext, 0, chained)
        carry_mid[rho * blk:(rho + 1) * blk, :] = kv_ext[ATT_STEP_ROWS:ATT_STEP_ROWS + blk, :]
        for c in range(rep):
            o, lse = _merge_lse(acc_o[c, reg, :], acc_lse[c, reg, :], step_o[c], step_lse[c], True)
            acc_o[c, reg, :] = o
            acc_lse[c, reg, :] = lse

        far = [pl.ds(a, blk, stride=d_mid) for a in range(ATT_STEP_BLOCKS)]
        project(2, jnp.concatenate([rows_of(xg, f) for f in far], axis=0).astype(BF16),
                jnp.concatenate([cos_g[f, :] for f in far], axis=0), jnp.concatenate([sin_g[f, :] for f in far], axis=0))
        run_blocks(carry_far, rho * ATT_STEP_ROWS, lambda a: seen_tile)
        carry_far[rho * ATT_STEP_ROWS:(rho + 1) * ATT_STEP_ROWS, :] = kv_ext[blk:blk + ATT_STEP_ROWS, :]
        for a in range(ATT_STEP_BLOCKS):
            dst = pl.ds(rho * region + a, blk, stride=d_mid)
            blk_rows = slice(a * blk, (a + 1) * blk)
            for c in range(rep):
                o, _ = _merge_lse(acc_o[c, dst, :], acc_lse[c, dst, :], step_o[c, blk_rows, :], step_lse[c, blk_rows, :], False)
                acc_o[c, dst, :] = o

        for c in range(rep):
            o_ref[c, src, :] = acc_o[c, reg, :]


def _attention(h_slabs, pos2d, w_qkv, batch, seq):
    t, d = h_slabs.shape[1], w_qkv.shape[1]
    gw = ATT_GROUP_WIDTH
    assert seq % ATT_TILE == 0
    tiles = seq // ATT_TILE
    tile = lambda b, n: b * tiles + n
    kv_rows = ATT_BLOCK + ATT_STEP_ROWS
    d_mid, d_far = ATT_GROUPS[1][1], ATT_GROUPS[2][1]
    return pl.pallas_call(
        _attn_fused_kernel,
        grid=(batch, tiles),
        in_specs=[
            _slab_spec(ATT_TILE, d, tile, pipeline_mode=pl.Buffered(1)),
            pl.BlockSpec((ATT_TILE, 1), lambda b, n: (tile(b, n), 0)),
            _resident(w_qkv.shape),
        ],
        out_specs=_slab_spec(ATT_TILE, gw, tile, pipeline_mode=pl.Buffered(1)),
        out_shape=jax.ShapeDtypeStruct((gw // LANES, t, LANES), F32),
        scratch_shapes=[
            pltpu.VMEM((ATT_TILE, LANES), F32),
            pltpu.VMEM((ATT_TILE, LANES), F32),
            pltpu.VMEM((d // LANES, ATT_STEP_ROWS, LANES), F32),
            pltpu.VMEM((ATT_STEP_ROWS, LANES), F32),
            pltpu.VMEM((ATT_STEP_ROWS, LANES), F32),
            pltpu.VMEM((ATT_STEP_ROWS, gw), BF16),
            pltpu.VMEM((kv_rows, 2 * gw), BF16),
            pltpu.VMEM((ATT_BLOCK, 2 * gw), BF16),
            pltpu.VMEM((d_mid * ATT_BLOCK, 2 * gw), BF16),
            pltpu.VMEM((d_far * ATT_BLOCK, 2 * gw), BF16),
            pltpu.VMEM((gw // LANES, ATT_STEP_ROWS, LANES), F32),
            pltpu.VMEM((gw // LANES, ATT_STEP_ROWS, LANES), F32),
            pltpu.VMEM((gw // LANES, ATT_TILE, LANES), F32),
            pltpu.VMEM((gw // LANES, ATT_TILE, LANES), F32),
        ],
        compiler_params=_params("parallel", "arbitrary"),
        name="attn_fused",
    )(h_slabs, pos2d, w_qkv)


def _split2(x):
    hi = x.astype(BF16)
    lo = (x - hi.astype(F32)).astype(BF16)
    return hi, lo


def _rows_bcast(x, row_in_block):
    parts = [jnp.broadcast_to(x[i + row_in_block:i + row_in_block + 1, :], (SUBLANES, x.shape[1]))
             for i in range(0, x.shape[0], SUBLANES)]
    return jnp.concatenate(parts, axis=0)


def _hgrn_kernel(lbl_ref, ng_ref, x_ref, w_ref, o_ref, z_s, *st_refs, layer, n_chunks):
    c = HGRN_CHUNK
    width = HGRN_HEADS * HGRN_DIM
    blk = SUBLANES
    nblk = c // blk

    @pl.when(pl.program_id(1) == 0)
    def _():
        for st_ref in st_refs:
            st_ref[...] = jnp.zeros_like(st_ref)

    xb = _load_slabs(x_ref).astype(BF16)
    for j in range(w_ref.shape[1] // PROJ_CHUNK):
        sl = slice(j * PROJ_CHUNK, (j + 1) * PROJ_CHUNK)
        z_s[:, sl] = _dot(xb, w_ref[:, sl])

    lgt = lbl_ref[...]
    ex = jnp.exp(lgt - jnp.max(lgt, axis=0, keepdims=True))
    sm = ex / jnp.sum(ex, axis=0, keepdims=True)
    lb = jnp.sum(sm[0:layer + 1, :], axis=0, keepdims=True)

    r_i = lax.broadcasted_iota(jnp.int32, (c, c), 0)
    c_i = lax.broadcasted_iota(jnp.int32, (c, c), 1)
    tri = (c_i <= r_i).astype(BF16)
    rowid = lax.broadcasted_iota(jnp.int32, (c, 1), 0)
    second = {hs: (rowid & hs) != 0 for hs in (1, 2, 4)}
    pair = {hs: jnp.logical_and((r_i // (2 * hs)) == (c_i // (2 * hs)),
                                jnp.logical_and((r_i & hs) != 0, (c_i & hs) == 0)) for hs in (1, 2, 4)}
    diag = r_i == c_i
    ng = ng_ref[...]
    zeros_blk = jnp.zeros((blk, HGRN_DIM), F32)

    def chunk_body(ci, carry):
        r0 = pl.multiple_of(ci * c, c)
        rows = pl.ds(r0, c)
        f = lb + (1.0 - lb) * jax.nn.sigmoid(z_s[rows, width:2 * width])
        lg = jnp.log(f)
        k = 1.0 - f
        q = _silu(z_s[rows, 0:width])
        v = z_s[rows, 2 * width:3 * width]
        hi, lo = _split2(lg)
        b = _dot(tri, hi) + _dot(tri, lo)
        b_last = b[c - 1:c, :]
        qs = (q * jnp.exp(b)).astype(BF16)
        ks = (k * jnp.exp(b_last - b)).astype(BF16)
        decay = jnp.exp(b_last)
        qb, kb, vb = q.astype(BF16), k.astype(BF16), v.astype(BF16)

        q_far, k_far = [], []
        for j in range(nblk - 1):
            lo_r = (j + 1) * blk
            b_end = b[lo_r - 1:lo_r, :]
            q_far.append(q[lo_r:, :] * jnp.exp(b[lo_r:, :] - b_end))
            k_far.append(k[lo_r - blk:lo_r, :] * jnp.exp(b_end - b[lo_r - blk:lo_r, :]))

        lvl_q, lvl_k = {}, {}
        for hs, bref in ((4, _rows_bcast(b, 3)),
                         (2, jnp.where(second[4], _rows_bcast(b, 5), _rows_bcast(b, 1)))):
            x = jnp.exp(-jnp.abs(b - bref))
            lvl_q[hs] = jnp.where(second[hs], q * x, 0.0).astype(BF16)
            lvl_k[hs] = jnp.where(second[hs], 0.0, k * x).astype(BF16)
        lvl_q[1] = jnp.where(second[1], q * f, 0.0).astype(BF16)
        lvl_k[1] = jnp.where(second[1], 0.0, k).astype(BF16)

        heads = [slice(h * HGRN_DIM, (h + 1) * HGRN_DIM) for h in range(HGRN_HEADS)]
        o_inter, att_all = [], []
        for h, sl in enumerate(heads):
            st = st_refs[h][...]
            o_inter.append(_dot_nt(qs[:, sl], st.astype(BF16)))
            q_cat = jnp.concatenate(
                [jnp.concatenate([jnp.zeros(((j + 1) * blk, HGRN_DIM), F32), q_far[j][:, sl]], axis=0)
                 for j in range(nblk - 1)], axis=1).astype(BF16)
            k_cat = jnp.concatenate(
                [jnp.concatenate([zeros_blk] * j + [k_far[j][:, sl]] + [zeros_blk] * (nblk - 1 - j), axis=0)
                 for j in range(nblk - 1)], axis=1).astype(BF16)
            att = _dot_nt(q_cat, k_cat)
            for hs in (4, 2, 1):
                att = att + jnp.where(pair[hs], _dot_nt(lvl_q[hs][:, sl], lvl_k[hs][:, sl]), 0.0)
            att = att + jnp.where(diag, _dot_nt(qb[:, sl], kb[:, sl]), 0.0)
            att_all.append(att.astype(BF16))
            st_refs[h][...] = st * decay[:, sl] + _dot_tn(vb[:, sl], ks[:, sl])
        for h, sl in enumerate(heads):
            o = o_inter[h] + _dot(att_all[h], vb[:, sl])
            o = o * lax.rsqrt(jnp.mean(o * o, axis=1, keepdims=True) + RMS_EPS)
            o = o * ng[:, sl] * _silu(z_s[rows, 3 * width + h * HGRN_DIM:3 * width + (h + 1) * HGRN_DIM])
            o_ref[rows, sl] = o.astype(o_ref.dtype)
        return carry

    lax.fori_loop(0, n_chunks, chunk_body, 0)


def _hgrn_scan(h_slabs, w_hgrn, lb_logits, norm_g, layer, batch, seq):
    width = HGRN_HEADS * HGRN_DIM
    d = w_hgrn.shape[0]
    ts = min(HGRN_STEP_TOKENS, seq)
    steps = seq // ts
    body = functools.partial(_hgrn_kernel, layer=layer, n_chunks=ts // HGRN_CHUNK)
    return pl.pallas_call(
        body,
        grid=(batch, steps),
        in_specs=[
            _resident(lb_logits.shape),
            _resident(norm_g.shape),
            _slab_spec(ts, d, lambda b, s: b * steps + s),
            _resident(w_hgrn.shape),
        ],
        out_specs=pl.BlockSpec((ts, width), lambda b, s: (b * steps + s, 0)),
        out_shape=jax.ShapeDtypeStruct((batch * seq, width), BF16),
        scratch_shapes=[pltpu.VMEM((ts, w_hgrn.shape[1]), F32)] + [pltpu.VMEM((HGRN_DIM, HGRN_DIM), F32)] * HGRN_HEADS,
        compiler_params=_params("parallel", "arbitrary"),
        name="hgrn_scan",
    )(lb_logits, norm_g, h_slabs, w_hgrn)


def _merge_kernel(h_ref, oa_ref, oh_ref, wg_ref, wpa_ref, wpb_ref, wo_ref, g_ref, b_ref, out_ref, *, alpha):
    h = _load_slabs(h_ref)
    d = h.shape[1]
    hb = h.astype(BF16)
    y_att = _dot(_load_slabs(oa_ref).astype(BF16), wpa_ref[...])
    y_hgrn = _dot(oh_ref[...], wpb_ref[...])
    merged = (jax.nn.sigmoid(_dot(hb, wg_ref[:, 0:d])) * y_att + jax.nn.sigmoid(_dot(hb, wg_ref[:, d:2 * d])) * y_hgrn)
    mix = _dot(merged.astype(BF16), wo_ref[...])
    out_ref[...] = _layer_norm(alpha * h + mix, g_ref[...], b_ref[...])


def _merge(h_slabs, o_att, o_hgrn, w_gate, w_pa, w_pb, w_o, g, b, alpha):
    t, d = o_hgrn.shape
    tm = min(TOKEN_TILE, t)
    row = lambda i: (i, 0)
    in_specs = [_slab_spec(tm, d, lambda i: i),
                _slab_spec(tm, o_att.shape[0] * LANES, lambda i: i),
                pl.BlockSpec((tm, d), row)]
    in_specs += [_resident(a.shape) for a in (w_gate, w_pa, w_pb, w_o, g, b)]
    return pl.pallas_call(
        functools.partial(_merge_kernel, alpha=alpha),
        grid=(t // tm,),
        in_specs=in_specs,
        out_specs=pl.BlockSpec((tm, d), row),
        out_shape=jax.ShapeDtypeStruct((t, d), F32),
        compiler_params=_params("parallel"),
        name="merge_ln",
    )(h_slabs, o_att, o_hgrn, w_gate, w_pa, w_pb, w_o, g, b)


def kernel(x, p, positions, ffn1_w_in, ffn1_w_out, ln1_g, ln1_b, w_mix_in, hgrn_lb_logits, hgrn_norm_g, w_proj_attn, w_proj_hgrn, w_mix_out, ln2_g, ln2_b, ffn2_w_in, ffn2_w_out, ln3_g, ln3_b, ple_w_proj, ple_w_gate):
    batch, seq, d_model = x.shape
    depth = ffn1_w_in.shape[0]
    alpha = (2 * depth) ** 0.25
    width = HGRN_HEADS * HGRN_DIM
    gw = ATT_GROUP_WIDTH
    t = batch * seq
    h = x.reshape(t, d_model)
    pos2d = positions.reshape(t, 1)
    lb_logits = hgrn_lb_logits.astype(F32)
    vec = lambda a: a.reshape(1, -1).astype(F32)
    c_hgrn = 3 * ATT_WIDTH
    c_gate = c_hgrn + 4 * width
    pair_order, _ = _pair_lane_order()
    qk_order = jnp.asarray([p * LANES + src for p in range(gw // LANES) for src in pair_order], jnp.int32)
    for i in range(depth):
        w_in = w_mix_in[i]
        def group_cols(part, g):
            cols = w_in[:, part * ATT_WIDTH + g * gw:part * ATT_WIDTH + (g + 1) * gw]
            return cols[:, qk_order] if part < 2 else cols
        w_qkv = jnp.stack([jnp.concatenate([group_cols(part, g) for part in range(3)], axis=1)
                           for g in range(len(ATT_GROUPS))]).astype(BF16)
        w_hgrn = w_in[:, c_hgrn:c_gate].astype(BF16)
        w_gate = w_in[:, c_gate:].astype(BF16)

        h = _ffn_ln(h, ffn1_w_in[i].astype(BF16), ffn1_w_out[i].astype(BF16), vec(ln1_g[i]), vec(ln1_b[i]), alpha)
        for window, dilation in ATT_GROUPS:
            assert window == ATT_BLOCK * dilation
        o_att = _attention(h, pos2d, w_qkv, batch, seq)
        o_hgrn = _hgrn_scan(h, w_hgrn, lb_logits, vec(hgrn_norm_g[i]), i, batch, seq)
        h = _merge(h, o_att, o_hgrn, w_gate, w_proj_attn[i].astype(BF16), w_proj_hgrn[i].astype(BF16),
                   w_mix_out[i].astype(BF16), vec(ln2_g[i]), vec(ln2_b[i]), alpha)
        h = _ffn_ln(h, ffn2_w_in[i].astype(BF16), ffn2_w_out[i].astype(BF16), vec(ln3_g[i]), vec(ln3_b[i]), alpha,
                    ple=(p[i].reshape(t, -1), ple_w_gate[i].astype(BF16), ple_w_proj[i].astype(BF16)))
    return h.reshape(batch, seq, d_model)
```

```python
import functools
import math

import jax
import jax.numpy as jnp
from jax import lax
from jax.experimental import pallas as pl
from jax.experimental.pallas import tpu as pltpu

F32 = jnp.float32
BF16 = jnp.bfloat16

ATT_HEAD_DIM = 64
ATT_HEADS_PER_GROUP = 8
ATT_GROUPS = ((128, 1), (512, 4), (2048, 16))
ATT_GROUP_WIDTH = ATT_HEADS_PER_GROUP * ATT_HEAD_DIM
ATT_WIDTH = ATT_GROUP_WIDTH * len(ATT_GROUPS)
ATT_BLOCK = 128
ROPE_THETA = 500000.0
ROPE_DIM = ATT_HEAD_DIM // 4
HGRN_HEADS = 8
HGRN_DIM = 128
LN_EPS = 1e-5
RMS_EPS = 1e-6

LANES = 128
SUBLANES = 8
VMEM_LIMIT_BYTES = 56 * 1024 * 1024

TOKEN_TILE = 512
FF_CHUNK = 256
PROJ_CHUNK = 512
ATT_STEP_BLOCKS = 4
ATT_STEP_ROWS = ATT_STEP_BLOCKS * ATT_BLOCK
ATT_BLOCKS_PER_TRIP = 2
ATT_TILE = ATT_BLOCK * max(d for _, d in ATT_GROUPS)
ROPE_TABLE_ROWS = 256
HGRN_CHUNK = 64
HGRN_STEP_TOKENS = 512


def _params(*sem):
    return pltpu.CompilerParams(dimension_semantics=sem, vmem_limit_bytes=VMEM_LIMIT_BYTES)


def _resident(shape):
    nd = len(shape)
    return pl.BlockSpec(shape, lambda *_: (0,) * nd)


def _layer_norm(y, g, b):
    mu = jnp.mean(y, axis=-1, keepdims=True)
    yc = y - mu
    var = jnp.mean(yc * yc, axis=-1, keepdims=True)
    return yc * lax.rsqrt(var + LN_EPS) * g + b


def _silu(a):
    return a * jax.nn.sigmoid(a)


def _slab_spec(rows, d, index_map, **kw):
    return pl.BlockSpec((d // LANES, rows, LANES), lambda *i: (0, index_map(*i), 0), **kw)


def _load_slabs(ref, rows=slice(None)):
    return jnp.concatenate([ref[c, rows, :] for c in range(ref.shape[0])], axis=1)


def _store_slabs(ref, val, rows=slice(None)):
    for c in range(ref.shape[0]):
        ref[c, rows, :] = val[:, c * LANES:(c + 1) * LANES]


def _dot(a, b):
    return jnp.dot(a, b, preferred_element_type=F32)


def _dot_nt(a, b):
    return lax.dot_general(a, b, (((1,), (1,)), ((), ())), preferred_element_type=F32)


def _dot_tn(a, b):
    return lax.dot_general(a, b, (((0,), (0,)), ((), ())), preferred_element_type=F32)


def _ffn_core(x, win_ref, wout_ref, d_ff):
    xb = x.astype(BF16)
    acc = jnp.zeros(x.shape, F32)
    for c in range(d_ff // FF_CHUNK):
        lo = c * FF_CHUNK
        a = _dot(xb, win_ref[:, lo:lo + FF_CHUNK])
        u = _dot(xb, win_ref[:, d_ff + lo:d_ff + lo + FF_CHUNK])
        acc = acc + _dot((_silu(a) * u).astype(BF16), wout_ref[lo:lo + FF_CHUNK, :])
    return acc


def _ffn_ln_kernel(x_ref, win_ref, wout_ref, g_ref, b_ref, o_ref, *, d_ff, alpha):
    x = x_ref[...]
    y = alpha * x + 0.5 * _ffn_core(x, win_ref, wout_ref, d_ff)
    _store_slabs(o_ref, _layer_norm(y, g_ref[...], b_ref[...]))


def _ffn_ln_ple_kernel(x_ref, p_ref, win_ref, wout_ref, g_ref, b_ref, wpg_ref, wpp_ref, o_ref, *, d_ff, alpha):
    x = x_ref[...]
    y = alpha * x + 0.5 * _ffn_core(x, win_ref, wout_ref, d_ff)
    h = _layer_norm(y, g_ref[...], b_ref[...])
    gate = jax.nn.sigmoid(_dot(h.astype(BF16), wpg_ref[...]))
    emb = _dot(p_ref[...].astype(BF16), wpp_ref[...])
    o_ref[...] = h + gate * emb


def _ffn_ln(x2d, w_in, w_out, g, b, alpha, ple=None):
    t, d = x2d.shape
    d_ff = w_out.shape[0]
    tm = min(TOKEN_TILE, t)
    row = lambda i: (i, 0)
    in_specs = [pl.BlockSpec((tm, d), row)]
    args = [x2d]
    if ple is not None:
        p2d, w_pg, w_pp = ple
        in_specs.append(pl.BlockSpec((tm, p2d.shape[1]), row))
        args.append(p2d)
    in_specs += [_resident(w_in.shape), _resident(w_out.shape), _resident(g.shape), _resident(b.shape)]
    args += [w_in, w_out, g, b]
    if ple is None:
        body = functools.partial(_ffn_ln_kernel, d_ff=d_ff, alpha=alpha)
    else:
        in_specs += [_resident(w_pg.shape), _resident(w_pp.shape)]
        args += [w_pg, w_pp]
        body = functools.partial(_ffn_ln_ple_kernel, d_ff=d_ff, alpha=alpha)
    return pl.pallas_call(
        body,
        grid=(t // tm,),
        in_specs=in_specs,
        out_specs=_slab_spec(tm, d, lambda i: i) if ple is None else pl.BlockSpec((tm, d), row),
        out_shape=jax.ShapeDtypeStruct((d // LANES, t, LANES) if ple is None else (t, d), F32),
        compiler_params=_params("parallel"),
        name="ffn_ln" if ple is None else "ffn_ln_ple",
    )(*args)


def _pair_lane_order():
    hd, half = ATT_HEAD_DIM, ROPE_DIM // 2
    a, b = list(range(hd)), list(range(hd, 2 * hd))
    order = (a[:half] + b[:half] + a[ROPE_DIM:] + a[half:ROPE_DIM] + b[half:ROPE_DIM] + b[ROPE_DIM:])
    return order, [src < hd for src in order]


def _rope_tables(pos_f32):
    lane = lax.broadcasted_iota(jnp.int32, (1, LANES), 1)
    half = ROPE_DIM // 2
    j = lane & (ATT_HEAD_DIM - 1)
    fidx = (j & (half - 1)).astype(F32)
    inv_freq = jnp.exp(-math.log(ROPE_THETA) * fidx * (2.0 / ROPE_DIM))
    inv_freq = jnp.where(j < ROPE_DIM, inv_freq, 0.0)
    ang = pos_f32 * inv_freq
    return jnp.cos(ang), jnp.where(lane < ATT_HEAD_DIM, -1.0, 1.0) * jnp.sin(ang)


def _merge_lse(o_a, lse_a, o_b, lse_b, want_lse):
    m = jnp.maximum(lse_a, lse_b)
    w_a = jnp.exp(lse_a - m)
    w_b = jnp.exp(lse_b - m)
    den = w_a + w_b
    o = (w_a * o_a + w_b * o_b) / den
    return o, (m + jnp.log(den) if want_lse else None)


def _attn_fused_kernel(x_ref, pos_ref, w_ref, o_ref, cos_s, sin_s, xg, cos_g, sin_g, q_buf, kv_ext,
                       carry_near, carry_mid, carry_far, step_o, step_lse, acc_o, acc_lse):
    n = pl.program_id(1)
    blk, gw = ATT_BLOCK, ATT_GROUP_WIDTH
    (_, d_near), (_, d_mid), (_, d_far) = ATT_GROUPS
    region = ATT_TILE // d_mid
    assert d_near == 1 and d_far == d_mid * d_mid and region == ATT_STEP_ROWS and d_mid == ATT_STEP_BLOCKS

    @pl.when(n == 0)
    def _():
        carry_near[...] = jnp.zeros_like(carry_near)
        carry_mid[...] = jnp.zeros_like(carry_mid)
        carry_far[...] = jnp.zeros_like(carry_far)

    def fill_tables(i, carry):
        rows = pl.ds(pl.multiple_of(i * ROPE_TABLE_ROWS, ROPE_TABLE_ROWS), ROPE_TABLE_ROWS)
        cos, sin = _rope_tables(pos_ref[rows, :].astype(F32))
        cos_s[rows, :] = cos
        sin_s[rows, :] = sin
        return carry

    lax.fori_loop(0, ATT_TILE // ROPE_TABLE_ROWS, fill_tables, 0)

    row = lax.broadcasted_iota(jnp.int32, (blk, blk), 0)
    col = lax.broadcasted_iota(jnp.int32, (blk, blk), 1)
    neg_inf = jnp.float32(-jnp.inf)
    bias_own = jnp.where(col <= row, 0.0, neg_inf)
    bias_prev = jnp.where(col >= row, 0.0, neg_inf)
    lane = lax.broadcasted_iota(jnp.int32, (1, LANES), 1)
    first = lane < ATT_HEAD_DIM
    lane_j = lane & (ATT_HEAD_DIM - 1)
    half = ROPE_DIM // 2
    first_qk = jnp.logical_or(lane_j < half, jnp.logical_and(lane < ATT_HEAD_DIM, lane_j >= ROPE_DIM))
    ones_a = jnp.broadcast_to(jnp.where(first, 1.0, 0.0).astype(BF16), (blk, LANES))
    ones_b = jnp.broadcast_to(jnp.where(first, 0.0, 1.0).astype(BF16), (blk, LANES))
    rep = gw // LANES

    def project(group, xb, cos, sin):
        for part in range(3):
            z = _dot(xb, w_ref[group, :, part * gw:(part + 1) * gw])
            if part < 2:
                slabs = []
                for c in range(rep):
                    zc = z[:, c * LANES:(c + 1) * LANES]
                    slabs.append(zc * cos + pltpu.roll(zc, ATT_HEAD_DIM, 1) * sin)
                z = jnp.concatenate(slabs, axis=1)
            if part == 0:
                q_buf[...] = (z * ATT_HEAD_DIM ** -0.5).astype(BF16)
            else:
                kv_ext[blk:blk + ATT_STEP_ROWS, (part - 1) * gw:part * gw] = z.astype(BF16)

    def run_blocks(prev_ref, prev_base, prev_valid):
        def scores(a):
            off = pl.multiple_of(a * blk, blk)
            q = q_buf[pl.ds(off, blk), :]
            own = kv_ext[pl.ds(off + blk, blk), 0:gw]
            prev = prev_ref[pl.ds(prev_base + off, blk), 0:gw]
            bias = jnp.concatenate([jnp.where(prev_valid(a), bias_prev, neg_inf), bias_own], axis=1)
            zero = jnp.zeros((blk, LANES), BF16)
            e_all, m_all = [], []
            for p in range(rep):
                sl = slice(p * LANES, (p + 1) * LANES)
                q2 = q[:, sl]
                q_cat = jnp.concatenate([jnp.where(first_qk, q2, zero), jnp.where(first_qk, zero, q2)], axis=0)
                k_cat = jnp.concatenate([prev[:, sl], own[:, sl]], axis=0)
                s = _dot_nt(q_cat, k_cat)
                e_parts, m_parts = [], []
                for hd in range(2):
                    s_h = s[hd * blk:(hd + 1) * blk, :] + bias
                    m = jnp.max(s_h, axis=1, keepdims=True)
                    e_parts.append(jnp.exp(s_h - m).astype(BF16))
                    m_parts.append(m)
                e_all.append(jnp.concatenate(e_parts, axis=1))
                m_all.append(jnp.where(first, m_parts[0], m_parts[1]))
            return e_all, m_all

        def values(a, e_all, m_all):
            off = pl.multiple_of(a * blk, blk)
            own = kv_ext[pl.ds(off + blk, blk), gw:2 * gw]
            prev = prev_ref[pl.ds(prev_base + off, blk), gw:2 * gw]
            zero = jnp.zeros((blk, LANES), BF16)
            for p in range(rep):
                sl = slice(p * LANES, (p + 1) * LANES)
                vp2, vo2 = prev[:, sl], own[:, sl]
                v_cat = jnp.concatenate([
                    jnp.concatenate([jnp.where(first, vp2, zero), ones_a], axis=1),
                    jnp.concatenate([jnp.where(first, vo2, zero), ones_a], axis=1),
                    jnp.concatenate([jnp.where(first, zero, vp2), ones_b], axis=1),
                    jnp.concatenate([jnp.where(first, zero, vo2), ones_b], axis=1)], axis=0)
                r = _dot(e_all[p], v_cat)
                den = r[:, LANES:]
                step_o[p, pl.ds(off, blk), :] = r[:, :LANES] / den
                step_lse[p, pl.ds(off, blk), :] = m_all[p] + jnp.log(den)

        def body(i, carry):
            blocks = [ATT_BLOCKS_PER_TRIP * i + u for u in range(ATT_BLOCKS_PER_TRIP)]
            staged = [scores(a) for a in blocks]
            for a, (e_all, m_all) in zip(blocks, staged):
                values(a, e_all, m_all)
            return carry

        lax.fori_loop(0, ATT_STEP_BLOCKS // ATT_BLOCKS_PER_TRIP, body, 0)

    def rows_of(ref3, rows):
        return jnp.concatenate([ref3[c, rows, :] for c in range(ref3.shape[0])], axis=1)

    seen_tile = n > 0
    chained = lambda a: jnp.logical_or(a > 0, seen_tile)

    kv_ext[0:blk, :] = carry_near[...]
    for j in range(ATT_TILE // ATT_STEP_ROWS):
        rows = slice(j * ATT_STEP_ROWS, (j + 1) * ATT_STEP_ROWS)
        project(0, rows_of(x_ref, rows).astype(BF16), cos_s[rows, :], sin_s[rows, :])
        run_blocks(kv_ext, 0, (lambda a: True) if j > 0 else chained)
        kv_ext[0:blk, :] = kv_ext[ATT_STEP_ROWS:ATT_STEP_ROWS + blk, :]
        for rho in range(d_mid):
            src = pl.ds(rho, blk, stride=d_mid)
            dst = slice(rho * region + j * blk, rho * region + (j + 1) * blk)
            for c in range(rep):
                acc_o[c, dst, :] = step_o[c, src, :]
                acc_lse[c, dst, :] = step_lse[c, src, :]
    carry_near[...] = kv_ext[0:blk, :]

    for rho in range(d_mid):
        src = pl.ds(rho, region, stride=d_mid)
        reg = slice(rho * region, (rho + 1) * region)
        for c in range(xg.shape[0]):
            xg[c] = x_ref[c, src, :]
        cos_g[...] = cos_s[src, :]
        sin_g[...] = sin_s[src, :]

        kv_ext[0:blk, :] = carry_mid[rho * blk:(rho + 1) * blk, :]
        project(1, rows_of(xg, slice(None)).astype(BF16), cos_g[...], sin_g[...])
        run_blocks(kv_ext, 0, chained)
        carry_mid[rho * blk:(rho + 1) * blk, :] = kv_ext[ATT_STEP_ROWS:ATT_STEP_ROWS + blk, :]
        for c in range(rep):
            o, lse = _merge_lse(acc_o[c, reg, :], acc_lse[c, reg, :], step_o[c], step_lse[c], True)
            acc_o[c, reg, :] = o
            acc_lse[c, reg, :] = lse

        far = [pl.ds(a, blk, stride=d_mid) for a in range(ATT_STEP_BLOCKS)]
        project(2, jnp.concatenate([rows_of(xg, f) for f in far], axis=0).astype(BF16),
                jnp.concatenate([cos_g[f, :] for f in far], axis=0), jnp.concatenate([sin_g[f, :] for f in far], axis=0))
        run_blocks(carry_far, rho * ATT_STEP_ROWS, lambda a: seen_tile)
        carry_far[rho * ATT_STEP_ROWS:(rho + 1) * ATT_STEP_ROWS, :] = kv_ext[blk:blk + ATT_STEP_ROWS, :]
        for a in range(ATT_STEP_BLOCKS):
            dst = pl.ds(rho * region + a, blk, stride=d_mid)
            blk_rows = slice(a * blk, (a + 1) * blk)
            for c in range(rep):
                o, _ = _merge_lse(acc_o[c, dst, :], acc_lse[c, dst, :], step_o[c, blk_rows, :], step_lse[c, blk_rows, :], False)
                acc_o[c, dst, :] = o

        for c in range(rep):
            o_ref[c, src, :] = acc_o[c, reg, :]


def _attention(h_slabs, pos2d, w_qkv, batch, seq):
    t, d = h_slabs.shape[1], w_qkv.shape[1]
    gw = ATT_GROUP_WIDTH
    assert seq % ATT_TILE == 0
    tiles = seq // ATT_TILE
    tile = lambda b, n: b * tiles + n
    kv_rows = ATT_BLOCK + ATT_STEP_ROWS
    d_mid, d_far = ATT_GROUPS[1][1], ATT_GROUPS[2][1]
    return pl.pallas_call(
        _attn_fused_kernel,
        grid=(batch, tiles),
        in_specs=[
            _slab_spec(ATT_TILE, d, tile, pipeline_mode=pl.Buffered(1)),
            pl.BlockSpec((ATT_TILE, 1), lambda b, n: (tile(b, n), 0)),
            _resident(w_qkv.shape),
        ],
        out_specs=_slab_spec(ATT_TILE, gw, tile, pipeline_mode=pl.Buffered(1)),
        out_shape=jax.ShapeDtypeStruct((gw // LANES, t, LANES), F32),
        scratch_shapes=[
            pltpu.VMEM((ATT_TILE, LANES), F32),
            pltpu.VMEM((ATT_TILE, LANES), F32),
            pltpu.VMEM((d // LANES, ATT_STEP_ROWS, LANES), F32),
            pltpu.VMEM((ATT_STEP_ROWS, LANES), F32),
            pltpu.VMEM((ATT_STEP_ROWS, LANES), F32),
            pltpu.VMEM((ATT_STEP_ROWS, gw), BF16),
            pltpu.VMEM((kv_rows, 2 * gw), BF16),
            pltpu.VMEM((ATT_BLOCK, 2 * gw), BF16),
            pltpu.VMEM((d_mid * ATT_BLOCK, 2 * gw), BF16),
            pltpu.VMEM((d_far * ATT_BLOCK, 2 * gw), BF16),
            pltpu.VMEM((gw // LANES, ATT_STEP_ROWS, LANES), F32),
            pltpu.VMEM((gw // LANES, ATT_STEP_ROWS, LANES), F32),
            pltpu.VMEM((gw // LANES, ATT_TILE, LANES), F32),
            pltpu.VMEM((gw // LANES, ATT_TILE, LANES), F32),
        ],
        compiler_params=_params("parallel", "arbitrary"),
        name="attn_fused",
    )(h_slabs, pos2d, w_qkv)


def _split2(x):
    hi = x.astype(BF16)
    lo = (x - hi.astype(F32)).astype(BF16)
    return hi, lo


def _rows_bcast(x, row_in_block):
    parts = [jnp.broadcast_to(x[i + row_in_block:i + row_in_block + 1, :], (SUBLANES, x.shape[1]))
             for i in range(0, x.shape[0], SUBLANES)]
    return jnp.concatenate(parts, axis=0)


def _hgrn_kernel(lbl_ref, ng_ref, x_ref, w_ref, o_ref, z_s, *st_refs, layer, n_chunks):
    c = HGRN_CHUNK
    width = HGRN_HEADS * HGRN_DIM
    blk = SUBLANES
    nblk = c // blk
    kinds = {"q": 0, "f": 1, "i": 2, "g": 3}

    @pl.when(pl.program_id(1) == 0)
    def _():
        for st_ref in st_refs:
            st_ref[...] = jnp.zeros_like(st_ref)

    lgt = lbl_ref[...]
    ex = jnp.exp(lgt - jnp.max(lgt, axis=0, keepdims=True))
    sm = ex / jnp.sum(ex, axis=0, keepdims=True)
    lb = jnp.sum(sm[0:layer + 1, :], axis=0, keepdims=True)

    r_i = lax.broadcasted_iota(jnp.int32, (c, c), 0)
    c_i = lax.broadcasted_iota(jnp.int32, (c, c), 1)
    tri = (c_i <= r_i).astype(BF16)
    rowid = lax.broadcasted_iota(jnp.int32, (c, 1), 0)
    second = {hs: (rowid & hs) != 0 for hs in (1, 2, 4)}
    pair = {hs: jnp.logical_and((r_i // (2 * hs)) == (c_i // (2 * hs)),
                                jnp.logical_and((r_i & hs) != 0, (c_i & hs) == 0)) for hs in (1, 2, 4)}
    diag = r_i == c_i
    ng = ng_ref[...]
    zeros_blk = jnp.zeros((blk, HGRN_DIM), F32)
    heads = [slice(h * HGRN_DIM, (h + 1) * HGRN_DIM) for h in range(HGRN_HEADS)]

    xb = _load_slabs(x_ref).astype(BF16)

    def project(kind):
        lo = kinds[kind] * width
        for j in range(width // PROJ_CHUNK):
            sl = slice(lo + j * PROJ_CHUNK, lo + (j + 1) * PROJ_CHUNK)
            z_s[:, sl] = _dot(xb, w_ref[:, sl])

    def pre(kind, ci):
        lo = kinds[kind] * width
        return z_s[ci * c:(ci + 1) * c, lo:lo + width]

    def decay_part(ci):
        f = lb + (1.0 - lb) * jax.nn.sigmoid(pre("f", ci))
        hi, lo = _split2(jnp.log(f))
        b = _dot(tri, hi) + _dot(tri, lo)
        return f, b

    def chunk(ci, f, b):
        rows = slice(ci * c, (ci + 1) * c)
        k = 1.0 - f
        q = _silu(pre("q", ci))
        v = pre("i", ci)
        b_last = b[c - 1:c, :]
        qs = (q * jnp.exp(b)).astype(BF16)
        ks = (k * jnp.exp(b_last - b)).astype(BF16)
        decay = jnp.exp(b_last)
        qb, kb, vb = q.astype(BF16), k.astype(BF16), v.astype(BF16)

        q_far, k_far = [], []
        for j in range(nblk - 1):
            lo_r = (j + 1) * blk
            b_end = b[lo_r - 1:lo_r, :]
            q_far.append(q[lo_r:, :] * jnp.exp(b[lo_r:, :] - b_end))
            k_far.append(k[lo_r - blk:lo_r, :] * jnp.exp(b_end - b[lo_r - blk:lo_r, :]))

        lvl_q, lvl_k = {}, {}
        for hs, bref in ((4, _rows_bcast(b, 3)),
                         (2, jnp.where(second[4], _rows_bcast(b, 5), _rows_bcast(b, 1)))):
            x = jnp.exp(-jnp.abs(b - bref))
            lvl_q[hs] = jnp.where(second[hs], q * x, 0.0).astype(BF16)
            lvl_k[hs] = jnp.where(second[hs], 0.0, k * x).astype(BF16)
        lvl_q[1] = jnp.where(second[1], q * f, 0.0).astype(BF16)
        lvl_k[1] = jnp.where(second[1], 0.0, k).astype(BF16)

        o_inter, att_all = [], []
        for h, sl in enumerate(heads):
            st = st_refs[h][...]
            o_inter.append(_dot_nt(qs[:, sl], st.astype(BF16)))
            q_cat = jnp.concatenate(
                [jnp.concatenate([jnp.zeros(((j + 1) * blk, HGRN_DIM), F32), q_far[j][:, sl]], axis=0)
                 for j in range(nblk - 1)], axis=1).astype(BF16)
            k_cat = jnp.concatenate(
                [jnp.concatenate([zeros_blk] * j + [k_far[j][:, sl]] + [zeros_blk] * (nblk - 1 - j), axis=0)
                 for j in range(nblk - 1)], axis=1).astype(BF16)
            att = _dot_nt(q_cat, k_cat)
            for hs in (4, 2, 1):
                att = att + jnp.where(pair[hs], _dot_nt(lvl_q[hs][:, sl], lvl_k[hs][:, sl]), 0.0)
            att = att + jnp.where(diag, _dot_nt(qb[:, sl], kb[:, sl]), 0.0)
            att_all.append(att.astype(BF16))
            st_refs[h][...] = st * decay[:, sl] + _dot_tn(vb[:, sl], ks[:, sl])
        g_act = _silu(pre("g", ci))
        for h, sl in enumerate(heads):
            o = o_inter[h] + _dot(att_all[h], vb[:, sl])
            o = o * lax.rsqrt(jnp.mean(o * o, axis=1, keepdims=True) + RMS_EPS)
            o_ref[rows, sl] = (o * ng[:, sl] * g_act[:, sl]).astype(o_ref.dtype)

    project("f")
    decays = [decay_part(ci) for ci in range(n_chunks)]
    project("q")
    project("i")
    project("g")
    for ci, (f, b) in enumerate(decays):
        chunk(ci, f, b)


def _hgrn_scan(h_slabs, w_hgrn, lb_logits, norm_g, layer, batch, seq):
    width = HGRN_HEADS * HGRN_DIM
    d = w_hgrn.shape[0]
    ts = min(HGRN_STEP_TOKENS, seq)
    steps = seq // ts
    body = functools.partial(_hgrn_kernel, layer=layer, n_chunks=ts // HGRN_CHUNK)
    return pl.pallas_call(
        body,
        grid=(batch, steps),
        in_specs=[
            _resident(lb_logits.shape),
            _resident(norm_g.shape),
            _slab_spec(ts, d, lambda b, s: b * steps + s),
            _resident(w_hgrn.shape),
        ],
        out_specs=pl.BlockSpec((ts, width), lambda b, s: (b * steps + s, 0)),
        out_shape=jax.ShapeDtypeStruct((batch * seq, width), BF16),
        scratch_shapes=[pltpu.VMEM((ts, w_hgrn.shape[1]), F32)] + [pltpu.VMEM((HGRN_DIM, HGRN_DIM), F32)] * HGRN_HEADS,
        compiler_params=_params("parallel", "arbitrary"),
        name="hgrn_scan",
    )(lb_logits, norm_g, h_slabs, w_hgrn)


def _merge_kernel(h_ref, oa_ref, oh_ref, wg_ref, wpa_ref, wpb_ref, wo_ref, g_ref, b_ref, out_ref, *, alpha):
    h = _load_slabs(h_ref)
    d = h.shape[1]
    hb = h.astype(BF16)
    y_att = _dot(_load_slabs(oa_ref).astype(BF16), wpa_ref[...])
    y_hgrn = _dot(oh_ref[...], wpb_ref[...])
    merged = (jax.nn.sigmoid(_dot(hb, wg_ref[:, 0:d])) * y_att + jax.nn.sigmoid(_dot(hb, wg_ref[:, d:2 * d])) * y_hgrn)
    mix = _dot(merged.astype(BF16), wo_ref[...])
    out_ref[...] = _layer_norm(alpha * h + mix, g_ref[...], b_ref[...])


def _merge(h_slabs, o_att, o_hgrn, w_gate, w_pa, w_pb, w_o, g, b, alpha):
    t, d = o_hgrn.shape
    tm = min(TOKEN_TILE, t)
    row = lambda i: (i, 0)
    in_specs = [_slab_spec(tm, d, lambda i: i),
                _slab_spec(tm, o_att.shape[0] * LANES, lambda i: i),
                pl.BlockSpec((tm, d), row)]
    in_specs += [_resident(a.shape) for a in (w_gate, w_pa, w_pb, w_o, g, b)]
    return pl.pallas_call(
        functools.partial(_merge_kernel, alpha=alpha),
        grid=(t // tm,),
        in_specs=in_specs,
        out_specs=pl.BlockSpec((tm, d), row),
        out_shape=jax.ShapeDtypeStruct((t, d), F32),
        compiler_params=_params("parallel"),
        name="merge_ln",
    )(h_slabs, o_att, o_hgrn, w_gate, w_pa, w_pb, w_o, g, b)


def kernel(x, p, positions, ffn1_w_in, ffn1_w_out, ln1_g, ln1_b, w_mix_in, hgrn_lb_logits, hgrn_norm_g, w_proj_attn, w_proj_hgrn, w_mix_out, ln2_g, ln2_b, ffn2_w_in, ffn2_w_out, ln3_g, ln3_b, ple_w_proj, ple_w_gate):
    batch, seq, d_model = x.shape
    depth = ffn1_w_in.shape[0]
    alpha = (2 * depth) ** 0.25
    width = HGRN_HEADS * HGRN_DIM
    gw = ATT_GROUP_WIDTH
    t = batch * seq
    h = x.reshape(t, d_model)
    pos2d = positions.reshape(t, 1)
    lb_logits = hgrn_lb_logits.astype(F32)
    vec = lambda a: a.reshape(1, -1).astype(F32)
    c_hgrn = 3 * ATT_WIDTH
    c_gate = c_hgrn + 4 * width
    pair_order, _ = _pair_lane_order()
    qk_order = jnp.asarray([p * LANES + src for p in range(gw // LANES) for src in pair_order], jnp.int32)
    for i in range(depth):
        w_in = w_mix_in[i]
        def group_cols(part, g):
            cols = w_in[:, part * ATT_WIDTH + g * gw:part * ATT_WIDTH + (g + 1) * gw]
            return cols[:, qk_order] if part < 2 else cols
        w_qkv = jnp.stack([jnp.concatenate([group_cols(part, g) for part in range(3)], axis=1)
                           for g in range(len(ATT_GROUPS))]).astype(BF16)
        w_hgrn = w_in[:, c_hgrn:c_gate].astype(BF16)
        w_gate = w_in[:, c_gate:].astype(BF16)

        h = _ffn_ln(h, ffn1_w_in[i].astype(BF16), ffn1_w_out[i].astype(BF16), vec(ln1_g[i]), vec(ln1_b[i]), alpha)
        for window, dilation in ATT_GROUPS:
            assert window == ATT_BLOCK * dilation
        o_att = _attention(h, pos2d, w_qkv, batch, seq)
        o_hgrn = _hgrn_scan(h, w_hgrn, lb_logits, vec(hgrn_norm_g[i]), i, batch, seq)
        h = _merge(h, o_att, o_hgrn, w_gate, w_proj_attn[i].astype(BF16), w_proj_hgrn[i].astype(BF16),
                   w_mix_out[i].astype(BF16), vec(ln2_g[i]), vec(ln2_b[i]), alpha)
        h = _ffn_ln(h, ffn2_w_in[i].astype(BF16), ffn2_w_out[i].astype(BF16), vec(ln3_g[i]), vec(ln3_b[i]), alpha,
                    ple=(p[i].reshape(t, -1), ple_w_gate[i].astype(BF16), ple_w_proj[i].astype(BF16)))
    return h.reshape(batch, seq, d_model)
```

```python
import functools
import math

import jax
import jax.numpy as jnp
from jax import lax
from jax.experimental import pallas as pl
from jax.experimental.pallas import tpu as pltpu

F32 = jnp.float32
BF16 = jnp.bfloat16

ATT_HEAD_DIM = 64
ATT_HEADS_PER_GROUP = 8
ATT_GROUPS = ((128, 1), (512, 4), (2048, 16))
ATT_GROUP_WIDTH = ATT_HEADS_PER_GROUP * ATT_HEAD_DIM
ATT_WIDTH = ATT_GROUP_WIDTH * len(ATT_GROUPS)
ATT_BLOCK = 128
ROPE_THETA = 500000.0
ROPE_DIM = ATT_HEAD_DIM // 4
HGRN_HEADS = 8
HGRN_DIM = 128
LN_EPS = 1e-5
RMS_EPS = 1e-6

LANES = 128
SUBLANES = 8
VMEM_LIMIT_BYTES = 60 * 1024 * 1024

TOKEN_TILE = 512
FF_CHUNK = 256
PROJ_CHUNK = 512
ATT_STEP_BLOCKS = 4
ATT_STEP_ROWS = ATT_STEP_BLOCKS * ATT_BLOCK
ATT_BLOCKS_PER_TRIP = 2
ATT_TILE = ATT_BLOCK * max(d for _, d in ATT_GROUPS)
HGRN_CHUNK = 64
HGRN_STEP_TOKENS = 512


def _params(*sem):
    return pltpu.CompilerParams(dimension_semantics=sem, vmem_limit_bytes=VMEM_LIMIT_BYTES)


def _resident(shape):
    nd = len(shape)
    return pl.BlockSpec(shape, lambda *_: (0,) * nd)


def _layer_norm(y, g, b):
    mu = jnp.mean(y, axis=-1, keepdims=True)
    yc = y - mu
    var = jnp.mean(yc * yc, axis=-1, keepdims=True)
    return yc * lax.rsqrt(var + LN_EPS) * g + b


def _silu(a):
    return a * jax.nn.sigmoid(a)


def _slab_spec(rows, d, index_map, **kw):
    return pl.BlockSpec((d // LANES, rows, LANES), lambda *i: (0, index_map(*i), 0), **kw)


def _load_slabs(ref, rows=slice(None)):
    return jnp.concatenate([ref[c, rows, :] for c in range(ref.shape[0])], axis=1)


def _store_slabs(ref, val, rows=slice(None)):
    for c in range(ref.shape[0]):
        ref[c, rows, :] = val[:, c * LANES:(c + 1) * LANES]


def _dot(a, b):
    return jnp.dot(a, b, preferred_element_type=F32)


def _dot_nt(a, b):
    return lax.dot_general(a, b, (((1,), (1,)), ((), ())), preferred_element_type=F32)


def _dot_tn(a, b):
    return lax.dot_general(a, b, (((0,), (0,)), ((), ())), preferred_element_type=F32)


def _ffn_core(x, win_ref, wout_ref, d_ff):
    xb = x.astype(BF16)
    acc = jnp.zeros(x.shape, F32)
    for c in range(d_ff // FF_CHUNK):
        lo = c * FF_CHUNK
        a = _dot(xb, win_ref[:, lo:lo + FF_CHUNK])
        u = _dot(xb, win_ref[:, d_ff + lo:d_ff + lo + FF_CHUNK])
        acc = acc + _dot((_silu(a) * u).astype(BF16), wout_ref[lo:lo + FF_CHUNK, :])
    return acc


def _ffn_ln_kernel(x_ref, win_ref, wout_ref, g_ref, b_ref, o_ref, *, d_ff, alpha):
    x = x_ref[...]
    y = alpha * x + 0.5 * _ffn_core(x, win_ref, wout_ref, d_ff)
    _store_slabs(o_ref, _layer_norm(y, g_ref[...], b_ref[...]))


def _ffn_ln_ple_kernel(x_ref, p_ref, win_ref, wout_ref, g_ref, b_ref, wpg_ref, wpp_ref, o_ref, *, d_ff, alpha):
    x = x_ref[...]
    y = alpha * x + 0.5 * _ffn_core(x, win_ref, wout_ref, d_ff)
    h = _layer_norm(y, g_ref[...], b_ref[...])
    gate = jax.nn.sigmoid(_dot(h.astype(BF16), wpg_ref[...]))
    emb = _dot(p_ref[...].astype(BF16), wpp_ref[...])
    o_ref[...] = h + gate * emb


def _ffn_ln(x2d, w_in, w_out, g, b, alpha, ple=None):
    t, d = x2d.shape
    d_ff = w_out.shape[0]
    tm = min(TOKEN_TILE, t)
    row = lambda i: (i, 0)
    in_specs = [pl.BlockSpec((tm, d), row)]
    args = [x2d]
    if ple is not None:
        p2d, w_pg, w_pp = ple
        in_specs.append(pl.BlockSpec((tm, p2d.shape[1]), row))
        args.append(p2d)
    in_specs += [_resident(w_in.shape), _resident(w_out.shape), _resident(g.shape), _resident(b.shape)]
    args += [w_in, w_out, g, b]
    if ple is None:
        body = functools.partial(_ffn_ln_kernel, d_ff=d_ff, alpha=alpha)
    else:
        in_specs += [_resident(w_pg.shape), _resident(w_pp.shape)]
        args += [w_pg, w_pp]
        body = functools.partial(_ffn_ln_ple_kernel, d_ff=d_ff, alpha=alpha)
    return pl.pallas_call(
        body,
        grid=(t // tm,),
        in_specs=in_specs,
        out_specs=_slab_spec(tm, d, lambda i: i) if ple is None else pl.BlockSpec((tm, d), row),
        out_shape=jax.ShapeDtypeStruct((d // LANES, t, LANES) if ple is None else (t, d), F32),
        compiler_params=_params("parallel"),
        name="ffn_ln" if ple is None else "ffn_ln_ple",
    )(*args)


def _pair_lane_order():
    hd, half = ATT_HEAD_DIM, ROPE_DIM // 2
    a, b = list(range(hd)), list(range(hd, 2 * hd))
    order = (a[:half] + b[:half] + a[ROPE_DIM:] + a[half:ROPE_DIM] + b[half:ROPE_DIM] + b[ROPE_DIM:])
    return order, [src < hd for src in order]


def _rope_tables(pos_row_f32):
    half = ROPE_DIM // 2
    assert half == SUBLANES
    freq = lax.broadcasted_iota(jnp.int32, (half, 1), 0).astype(F32)
    ang = jnp.exp(-math.log(ROPE_THETA) * freq * (2.0 / ROPE_DIM)) * pos_row_f32
    k_row = lax.broadcasted_iota(jnp.int32, (4 * half, LANES), 0)
    k_lane = lax.broadcasted_iota(jnp.int32, (4 * half, LANES), 1)
    take = jnp.logical_and((k_row & (half - 1)) == (k_lane & (half - 1)), k_row < 3 * half)
    spread_matrix = jnp.where(take, 1.0, 0.0).astype(BF16)

    def spread(c8):
        hi = c8.astype(BF16).astype(F32)
        mid = (c8 - hi).astype(BF16).astype(F32)
        lo = c8 - hi - mid
        terms = jnp.concatenate([hi, mid, lo, jnp.zeros_like(hi)], axis=0).astype(BF16)
        return _dot_tn(terms, spread_matrix)

    lane = lax.broadcasted_iota(jnp.int32, (1, LANES), 1)
    rotary = (lane & (ATT_HEAD_DIM - 1)) < ROPE_DIM
    sign = jnp.where(lane < ATT_HEAD_DIM, -1.0, 1.0)
    return jnp.where(rotary, spread(jnp.cos(ang)), 1.0), jnp.where(rotary, spread(jnp.sin(ang)) * sign, 0.0)


def _merge_lse(o_a, lse_a, o_b, lse_b, want_lse):
    m = jnp.maximum(lse_a, lse_b)
    w_a = jnp.exp(lse_a - m)
    w_b = jnp.exp(lse_b - m)
    den = w_a + w_b
    o = (w_a * o_a + w_b * o_b) / den
    return o, (m + jnp.log(den) if want_lse else None)


def _attn_fused_kernel(x_ref, pos_ref, w_ref, o_ref, cos_s, sin_s, xg, cos_g, sin_g, q_buf, kv_ext,
                       carry_near, carry_mid, carry_far, step_o, step_lse, acc_o, acc_lse):
    n = pl.program_id(1)
    blk, gw = ATT_BLOCK, ATT_GROUP_WIDTH
    (_, d_near), (_, d_mid), (_, d_far) = ATT_GROUPS
    region = ATT_TILE // d_mid
    assert d_near == 1 and d_far == d_mid * d_mid and region == ATT_STEP_ROWS and d_mid == ATT_STEP_BLOCKS

    @pl.when(n == 0)
    def _():
        carry_near[...] = jnp.zeros_like(carry_near)
        carry_mid[...] = jnp.zeros_like(carry_mid)
        carry_far[...] = jnp.zeros_like(carry_far)

    cos_s[...], sin_s[...] = _rope_tables(pos_ref[...].astype(F32))

    row = lax.broadcasted_iota(jnp.int32, (blk, blk), 0)
    col = lax.broadcasted_iota(jnp.int32, (blk, blk), 1)
    neg_inf = jnp.float32(-jnp.inf)
    bias_own = jnp.where(col <= row, 0.0, neg_inf)
    bias_prev = jnp.where(col >= row, 0.0, neg_inf)
    lane = lax.broadcasted_iota(jnp.int32, (1, LANES), 1)
    first = lane < ATT_HEAD_DIM
    lane_j = lane & (ATT_HEAD_DIM - 1)
    half = ROPE_DIM // 2
    first_qk = jnp.logical_or(lane_j < half, jnp.logical_and(lane < ATT_HEAD_DIM, lane_j >= ROPE_DIM))
    ones_a = jnp.broadcast_to(jnp.where(first, 1.0, 0.0).astype(BF16), (blk, LANES))
    ones_b = jnp.broadcast_to(jnp.where(first, 0.0, 1.0).astype(BF16), (blk, LANES))
    rep = gw // LANES

    def project(group, xb, cos, sin):
        for part in range(3):
            z = _dot(xb, w_ref[group, :, part * gw:(part + 1) * gw])
            if part < 2:
                slabs = []
                for c in range(rep):
                    zc = z[:, c * LANES:(c + 1) * LANES]
                    slabs.append(zc * cos + pltpu.roll(zc, ATT_HEAD_DIM, 1) * sin)
                z = jnp.concatenate(slabs, axis=1)
            if part == 0:
                q_buf[...] = (z * ATT_HEAD_DIM ** -0.5).astype(BF16)
            else:
                kv_ext[blk:blk + ATT_STEP_ROWS, (part - 1) * gw:part * gw] = z.astype(BF16)

    def run_blocks(prev_ref, prev_base, prev_valid):
        def scores(a):
            off = pl.multiple_of(a * blk, blk)
            q = q_buf[pl.ds(off, blk), :]
            own = kv_ext[pl.ds(off + blk, blk), 0:gw]
            prev = prev_ref[pl.ds(prev_base + off, blk), 0:gw]
            bias = jnp.concatenate([jnp.where(prev_valid(a), bias_prev, neg_inf), bias_own], axis=1)
            zero = jnp.zeros((blk, LANES), BF16)
            e_all, m_all = [], []
            for p in range(rep):
                sl = slice(p * LANES, (p + 1) * LANES)
                q2 = q[:, sl]
                q_cat = jnp.concatenate([jnp.where(first_qk, q2, zero), jnp.where(first_qk, zero, q2)], axis=0)
                k_cat = jnp.concatenate([prev[:, sl], own[:, sl]], axis=0)
                s = _dot_nt(q_cat, k_cat)
                e_parts, m_parts = [], []
                for hd in range(2):
                    s_h = s[hd * blk:(hd + 1) * blk, :] + bias
                    m = jnp.max(s_h, axis=1, keepdims=True)
                    e_parts.append(jnp.exp(s_h - m).astype(BF16))
                    m_parts.append(m)
                e_all.append(jnp.concatenate(e_parts, axis=1))
                m_all.append(jnp.where(first, m_parts[0], m_parts[1]))
            return e_all, m_all

        def values(a, e_all, m_all):
            off = pl.multiple_of(a * blk, blk)
            own = kv_ext[pl.ds(off + blk, blk), gw:2 * gw]
            prev = prev_ref[pl.ds(prev_base + off, blk), gw:2 * gw]
            zero = jnp.zeros((blk, LANES), BF16)
            for p in range(rep):
                sl = slice(p * LANES, (p + 1) * LANES)
                vp2, vo2 = prev[:, sl], own[:, sl]
                v_cat = jnp.concatenate([
                    jnp.concatenate([jnp.where(first, vp2, zero), ones_a], axis=1),
                    jnp.concatenate([jnp.where(first, vo2, zero), ones_a], axis=1),
                    jnp.concatenate([jnp.where(first, zero, vp2), ones_b], axis=1),
                    jnp.concatenate([jnp.where(first, zero, vo2), ones_b], axis=1)], axis=0)
                r = _dot(e_all[p], v_cat)
                den = r[:, LANES:]
                step_o[p, pl.ds(off, blk), :] = r[:, :LANES] / den
                step_lse[p, pl.ds(off, blk), :] = m_all[p] + jnp.log(den)

        def body(i, carry):
            blocks = [ATT_BLOCKS_PER_TRIP * i + u for u in range(ATT_BLOCKS_PER_TRIP)]
            staged = [scores(a) for a in blocks]
            for a, (e_all, m_all) in zip(blocks, staged):
                values(a, e_all, m_all)
            return carry

        lax.fori_loop(0, ATT_STEP_BLOCKS // ATT_BLOCKS_PER_TRIP, body, 0)

    def rows_of(ref3, rows):
        return jnp.concatenate([ref3[c, rows, :] for c in range(ref3.shape[0])], axis=1)

    seen_tile = n > 0
    chained = lambda a: jnp.logical_or(a > 0, seen_tile)

    kv_ext[0:blk, :] = carry_near[...]
    for j in range(ATT_TILE // ATT_STEP_ROWS):
        rows = slice(j * ATT_STEP_ROWS, (j + 1) * ATT_STEP_ROWS)
        project(0, rows_of(x_ref, rows).astype(BF16), cos_s[rows, :], sin_s[rows, :])
        run_blocks(kv_ext, 0, (lambda a: True) if j > 0 else chained)
        kv_ext[0:blk, :] = kv_ext[ATT_STEP_ROWS:ATT_STEP_ROWS + blk, :]
        for rho in range(d_mid):
            src = pl.ds(rho, blk, stride=d_mid)
            dst = slice(rho * region + j * blk, rho * region + (j + 1) * blk)
            for c in range(rep):
                acc_o[c, dst, :] = step_o[c, src, :]
                acc_lse[c, dst, :] = step_lse[c, src, :]
    carry_near[...] = kv_ext[0:blk, :]

    for rho in range(d_mid):
        src = pl.ds(rho, region, stride=d_mid)
        reg = slice(rho * region, (rho + 1) * region)
        for c in range(xg.shape[0]):
            xg[c] = x_ref[c, src, :]
        cos_g[...] = cos_s[src, :]
        sin_g[...] = sin_s[src, :]

        kv_ext[0:blk, :] = carry_mid[rho * blk:(rho + 1) * blk, :]
        project(1, rows_of(xg, slice(None)).astype(BF16), cos_g[...], sin_g[...])
        run_blocks(kv_ext, 0, chained)
        carry_mid[rho * blk:(rho + 1) * blk, :] = kv_ext[ATT_STEP_ROWS:ATT_STEP_ROWS + blk, :]
        for c in range(rep):
            o, lse = _merge_lse(acc_o[c, reg, :], acc_lse[c, reg, :], step_o[c], step_lse[c], True)
            acc_o[c, reg, :] = o
            acc_lse[c, reg, :] = lse

        far = [pl.ds(a, blk, stride=d_mid) for a in range(ATT_STEP_BLOCKS)]
        project(2, jnp.concatenate([rows_of(xg, f) for f in far], axis=0).astype(BF16),
                jnp.concatenate([cos_g[f, :] for f in far], axis=0), jnp.concatenate([sin_g[f, :] for f in far], axis=0))
        run_blocks(carry_far, rho * ATT_STEP_ROWS, lambda a: seen_tile)
        carry_far[rho * ATT_STEP_ROWS:(rho + 1) * ATT_STEP_ROWS, :] = kv_ext[blk:blk + ATT_STEP_ROWS, :]
        for a in range(ATT_STEP_BLOCKS):
            dst = pl.ds(rho * region + a, blk, stride=d_mid)
            blk_rows = slice(a * blk, (a + 1) * blk)
            for c in range(rep):
                o, _ = _merge_lse(acc_o[c, dst, :], acc_lse[c, dst, :], step_o[c, blk_rows, :], step_lse[c, blk_rows, :], False)
                acc_o[c, dst, :] = o

        for c in range(rep):
            o_ref[c, src, :] = acc_o[c, reg, :]


def _attention(h_slabs, pos_rows, w_qkv, batch, seq):
    t, d = h_slabs.shape[1], w_qkv.shape[1]
    gw = ATT_GROUP_WIDTH
    assert seq % ATT_TILE == 0
    tiles = seq // ATT_TILE
    tile = lambda b, n: b * tiles + n
    kv_rows = ATT_BLOCK + ATT_STEP_ROWS
    d_mid, d_far = ATT_GROUPS[1][1], ATT_GROUPS[2][1]
    return pl.pallas_call(
        _attn_fused_kernel,
        grid=(batch, tiles),
        in_specs=[
            _slab_spec(ATT_TILE, d, tile),
            pl.BlockSpec((None, 1, ATT_TILE), lambda b, n: (tile(b, n), 0, 0)),
            _resident(w_qkv.shape),
        ],
        out_specs=_slab_spec(ATT_TILE, gw, tile, pipeline_mode=pl.Buffered(1)),
        out_shape=jax.ShapeDtypeStruct((gw // LANES, t, LANES), F32),
        scratch_shapes=[
            pltpu.VMEM((ATT_TILE, LANES), F32),
            pltpu.VMEM((ATT_TILE, LANES), F32),
            pltpu.VMEM((d // LANES, ATT_STEP_ROWS, LANES), F32),
            pltpu.VMEM((ATT_STEP_ROWS, LANES), F32),
            pltpu.VMEM((ATT_STEP_ROWS, LANES), F32),
            pltpu.VMEM((ATT_STEP_ROWS, gw), BF16),
            pltpu.VMEM((kv_rows, 2 * gw), BF16),
            pltpu.VMEM((ATT_BLOCK, 2 * gw), BF16),
            pltpu.VMEM((d_mid * ATT_BLOCK, 2 * gw), BF16),
            pltpu.VMEM((d_far * ATT_BLOCK, 2 * gw), BF16),
            pltpu.VMEM((gw // LANES, ATT_STEP_ROWS, LANES), F32),
            pltpu.VMEM((gw // LANES, ATT_STEP_ROWS, LANES), F32),
            pltpu.VMEM((gw // LANES, ATT_TILE, LANES), F32),
            pltpu.VMEM((gw // LANES, ATT_TILE, LANES), F32),
        ],
        compiler_params=_params("parallel", "arbitrary"),
        name="attn_fused",
    )(h_slabs, pos_rows, w_qkv)


def _split2(x):
    hi = x.astype(BF16)
    lo = (x - hi.astype(F32)).astype(BF16)
    return hi, lo


def _rows_bcast(x, row_in_block):
    parts = [jnp.broadcast_to(x[i + row_in_block:i + row_in_block + 1, :], (SUBLANES, x.shape[1]))
             for i in range(0, x.shape[0], SUBLANES)]
    return jnp.concatenate(parts, axis=0)


def _hgrn_kernel(lbl_ref, ng_ref, x_ref, w_ref, o_ref, z_s, *st_refs, layer, n_chunks):
    c = HGRN_CHUNK
    width = HGRN_HEADS * HGRN_DIM
    blk = SUBLANES
    nblk = c // blk
    kinds = {"q": 0, "f": 1, "i": 2, "g": 3}

    @pl.when(pl.program_id(1) == 0)
    def _():
        for st_ref in st_refs:
            st_ref[...] = jnp.zeros_like(st_ref)

    lgt = lbl_ref[...]
    ex = jnp.exp(lgt - jnp.max(lgt, axis=0, keepdims=True))
    sm = ex / jnp.sum(ex, axis=0, keepdims=True)
    lb = jnp.sum(sm[0:layer + 1, :], axis=0, keepdims=True)

    r_i = lax.broadcasted_iota(jnp.int32, (c, c), 0)
    c_i = lax.broadcasted_iota(jnp.int32, (c, c), 1)
    tri = (c_i <= r_i).astype(BF16)
    rowid = lax.broadcasted_iota(jnp.int32, (c, 1), 0)
    second = {hs: (rowid & hs) != 0 for hs in (1, 2, 4)}
    pair = {hs: jnp.logical_and((r_i // (2 * hs)) == (c_i // (2 * hs)),
                                jnp.logical_and((r_i & hs) != 0, (c_i & hs) == 0)) for hs in (1, 2, 4)}
    diag = r_i == c_i
    ng = ng_ref[...]
    zeros_blk = jnp.zeros((blk, HGRN_DIM), F32)
    heads = [slice(h * HGRN_DIM, (h + 1) * HGRN_DIM) for h in range(HGRN_HEADS)]

    xb = _load_slabs(x_ref).astype(BF16)

    def project(kind):
        lo = kinds[kind] * width
        for j in range(width // PROJ_CHUNK):
            sl = slice(lo + j * PROJ_CHUNK, lo + (j + 1) * PROJ_CHUNK)
            z_s[:, sl] = _dot(xb, w_ref[:, sl])

    def pre(kind, ci):
        lo = kinds[kind] * width
        return z_s[ci * c:(ci + 1) * c, lo:lo + width]

    def decay_part(ci):
        f = lb + (1.0 - lb) * jax.nn.sigmoid(pre("f", ci))
        hi, lo = _split2(jnp.log(f))
        b = _dot(tri, hi) + _dot(tri, lo)
        return f, b

    def chunk(ci, f, b):
        rows = slice(ci * c, (ci + 1) * c)
        k = 1.0 - f
        q = _silu(pre("q", ci))
        v = pre("i", ci)
        b_last = b[c - 1:c, :]
        qs = (q * jnp.exp(b)).astype(BF16)
        ks = (k * jnp.exp(b_last - b)).astype(BF16)
        decay = jnp.exp(b_last)
        qb, kb, vb = q.astype(BF16), k.astype(BF16), v.astype(BF16)

        q_far, k_far = [], []
        for j in range(nblk - 1):
            lo_r = (j + 1) * blk
            b_end = b[lo_r - 1:lo_r, :]
            q_far.append(q[lo_r:, :] * jnp.exp(b[lo_r:, :] - b_end))
            k_far.append(k[lo_r - blk:lo_r, :] * jnp.exp(b_end - b[lo_r - blk:lo_r, :]))

        lvl_q, lvl_k = {}, {}
        for hs, bref in ((4, _rows_bcast(b, 3)),
                         (2, jnp.where(second[4], _rows_bcast(b, 5), _rows_bcast(b, 1)))):
            x = jnp.exp(-jnp.abs(b - bref))
            lvl_q[hs] = jnp.where(second[hs], q * x, 0.0).astype(BF16)
            lvl_k[hs] = jnp.where(second[hs], 0.0, k * x).astype(BF16)
        lvl_q[1] = jnp.where(second[1], q * f, 0.0).astype(BF16)
        lvl_k[1] = jnp.where(second[1], 0.0, k).astype(BF16)

        o_inter, att_all = [], []
        for h, sl in enumerate(heads):
            st = st_refs[h][...]
            o_inter.append(_dot_nt(qs[:, sl], st.astype(BF16)))
            q_cat = jnp.concatenate(
                [jnp.concatenate([jnp.zeros(((j + 1) * blk, HGRN_DIM), F32), q_far[j][:, sl]], axis=0)
                 for j in range(nblk - 1)], axis=1).astype(BF16)
            k_cat = jnp.concatenate(
                [jnp.concatenate([zeros_blk] * j + [k_far[j][:, sl]] + [zeros_blk] * (nblk - 1 - j), axis=0)
                 for j in range(nblk - 1)], axis=1).astype(BF16)
            att = _dot_nt(q_cat, k_cat)
            for hs in (4, 2, 1):
                att = att + jnp.where(pair[hs], _dot_nt(lvl_q[hs][:, sl], lvl_k[hs][:, sl]), 0.0)
            att = att + jnp.where(diag, _dot_nt(qb[:, sl], kb[:, sl]), 0.0)
            att_all.append(att.astype(BF16))
            st_refs[h][...] = st * decay[:, sl] + _dot_tn(vb[:, sl], ks[:, sl])
        g_act = _silu(pre("g", ci))
        for h, sl in enumerate(heads):
            o = o_inter[h] + _dot(att_all[h], vb[:, sl])
            o = o * lax.rsqrt(jnp.mean(o * o, axis=1, keepdims=True) + RMS_EPS)
            o_ref[rows, sl] = (o * ng[:, sl] * g_act[:, sl]).astype(o_ref.dtype)

    project("f")
    decays = [decay_part(ci) for ci in range(n_chunks)]
    project("q")
    project("i")
    project("g")
    for ci, (f, b) in enumerate(decays):
        chunk(ci, f, b)


def _hgrn_scan(h_slabs, w_hgrn, lb_logits, norm_g, layer, batch, seq):
    width = HGRN_HEADS * HGRN_DIM
    d = w_hgrn.shape[0]
    ts = min(HGRN_STEP_TOKENS, seq)
    steps = seq // ts
    body = functools.partial(_hgrn_kernel, layer=layer, n_chunks=ts // HGRN_CHUNK)
    return pl.pallas_call(
        body,
        grid=(batch, steps),
        in_specs=[
            _resident(lb_logits.shape),
            _resident(norm_g.shape),
            _slab_spec(ts, d, lambda b, s: b * steps + s),
            _resident(w_hgrn.shape),
        ],
        out_specs=pl.BlockSpec((ts, width), lambda b, s: (b * steps + s, 0)),
        out_shape=jax.ShapeDtypeStruct((batch * seq, width), BF16),
        scratch_shapes=[pltpu.VMEM((ts, w_hgrn.shape[1]), F32)] + [pltpu.VMEM((HGRN_DIM, HGRN_DIM), F32)] * HGRN_HEADS,
        compiler_params=_params("parallel", "arbitrary"),
        name="hgrn_scan",
    )(lb_logits, norm_g, h_slabs, w_hgrn)


def _merge_kernel(h_ref, oa_ref, oh_ref, wg_ref, wpa_ref, wpb_ref, wo_ref, g_ref, b_ref, out_ref, *, alpha):
    h = _load_slabs(h_ref)
    d = h.shape[1]
    hb = h.astype(BF16)
    y_att = _dot(_load_slabs(oa_ref).astype(BF16), wpa_ref[...])
    y_hgrn = _dot(oh_ref[...], wpb_ref[...])
    merged = (jax.nn.sigmoid(_dot(hb, wg_ref[:, 0:d])) * y_att + jax.nn.sigmoid(_dot(hb, wg_ref[:, d:2 * d])) * y_hgrn)
    mix = _dot(merged.astype(BF16), wo_ref[...])
    out_ref[...] = _layer_norm(alpha * h + mix, g_ref[...], b_ref[...])


def _merge(h_slabs, o_att, o_hgrn, w_gate, w_pa, w_pb, w_o, g, b, alpha):
    t, d = o_hgrn.shape
    tm = min(TOKEN_TILE, t)
    row = lambda i: (i, 0)
    in_specs = [_slab_spec(tm, d, lambda i: i),
                _slab_spec(tm, o_att.shape[0] * LANES, lambda i: i),
                pl.BlockSpec((tm, d), row)]
    in_specs += [_resident(a.shape) for a in (w_gate, w_pa, w_pb, w_o, g, b)]
    return pl.pallas_call(
        functools.partial(_merge_kernel, alpha=alpha),
        grid=(t // tm,),
        in_specs=in_specs,
        out_specs=pl.BlockSpec((tm, d), row),
        out_shape=jax.ShapeDtypeStruct((t, d), F32),
        compiler_params=_params("parallel"),
        name="merge_ln",
    )(h_slabs, o_att, o_hgrn, w_gate, w_pa, w_pb, w_o, g, b)


def kernel(x, p, positions, ffn1_w_in, ffn1_w_out, ln1_g, ln1_b, w_mix_in, hgrn_lb_logits, hgrn_norm_g, w_proj_attn, w_proj_hgrn, w_mix_out, ln2_g, ln2_b, ffn2_w_in, ffn2_w_out, ln3_g, ln3_b, ple_w_proj, ple_w_gate):
    batch, seq, d_model = x.shape
    depth = ffn1_w_in.shape[0]
    alpha = (2 * depth) ** 0.25
    width = HGRN_HEADS * HGRN_DIM
    gw = ATT_GROUP_WIDTH
    t = batch * seq
    h = x.reshape(t, d_model)
    pos_rows = positions.reshape(t // ATT_TILE, 1, ATT_TILE)
    lb_logits = hgrn_lb_logits.astype(F32)
    vec = lambda a: a.reshape(1, -1).astype(F32)
    c_hgrn = 3 * ATT_WIDTH
    c_gate = c_hgrn + 4 * width
    pair_order, _ = _pair_lane_order()
    qk_order = jnp.asarray([p * LANES + src for p in range(gw // LANES) for src in pair_order], jnp.int32)
    for i in range(depth):
        w_in = w_mix_in[i]
        def group_cols(part, g):
            cols = w_in[:, part * ATT_WIDTH + g * gw:part * ATT_WIDTH + (g + 1) * gw]
            return cols[:, qk_order] if part < 2 else cols
        w_qkv = jnp.stack([jnp.concatenate([group_cols(part, g) for part in range(3)], axis=1)
                           for g in range(len(ATT_GROUPS))]).astype(BF16)
        w_hgrn = w_in[:, c_hgrn:c_gate].astype(BF16)
        w_gate = w_in[:, c_gate:].astype(BF16)

        h = _ffn_ln(h, ffn1_w_in[i].astype(BF16), ffn1_w_out[i].astype(BF16), vec(ln1_g[i]), vec(ln1_b[i]), alpha)
        for window, dilation in ATT_GROUPS:
            assert window == ATT_BLOCK * dilation
        o_att = _attention(h, pos_rows, w_qkv, batch, seq)
        o_hgrn = _hgrn_scan(h, w_hgrn, lb_logits, vec(hgrn_norm_g[i]), i, batch, seq)
        h = _merge(h, o_att, o_hgrn, w_gate, w_proj_attn[i].astype(BF16), w_proj_hgrn[i].astype(BF16),
                   w_mix_out[i].astype(BF16), vec(ln2_g[i]), vec(ln2_b[i]), alpha)
        h = _ffn_ln(h, ffn2_w_in[i].astype(BF16), ffn2_w_out[i].astype(BF16), vec(ln3_g[i]), vec(ln3_b[i]), alpha,
                    ple=(p[i].reshape(t, -1), ple_w_gate[i].astype(BF16), ple_w_proj[i].astype(BF16)))
    return h.reshape(batch, seq, d_model)
```

```python
import functools
import math

import jax
import jax.numpy as jnp
from jax import lax
from jax.experimental import pallas as pl
from jax.experimental.pallas import tpu as pltpu

F32 = jnp.float32
BF16 = jnp.bfloat16

ATT_HEAD_DIM = 64
ATT_HEADS_PER_GROUP = 8
ATT_GROUPS = ((128, 1), (512, 4), (2048, 16))
ATT_GROUP_WIDTH = ATT_HEADS_PER_GROUP * ATT_HEAD_DIM
ATT_WIDTH = ATT_GROUP_WIDTH * len(ATT_GROUPS)
ATT_BLOCK = 128
ROPE_THETA = 500000.0
ROPE_DIM = ATT_HEAD_DIM // 4
HGRN_HEADS = 8
HGRN_DIM = 128
LN_EPS = 1e-5
RMS_EPS = 1e-6

LANES = 128
SUBLANES = 8
VMEM_LIMIT_BYTES = 60 * 1024 * 1024

TOKEN_TILE = 1024
FF_CHUNK = 256
PROJ_CHUNK = 512
ATT_STEP_BLOCKS = 4
ATT_STEP_ROWS = ATT_STEP_BLOCKS * ATT_BLOCK
ATT_BLOCKS_PER_TRIP = 2
ATT_TILE = ATT_BLOCK * max(d for _, d in ATT_GROUPS)
HGRN_CHUNK = 64
HGRN_STEP_TOKENS = 512


def _params(*sem):
    return pltpu.CompilerParams(dimension_semantics=sem, vmem_limit_bytes=VMEM_LIMIT_BYTES)


def _resident(shape):
    nd = len(shape)
    return pl.BlockSpec(shape, lambda *_: (0,) * nd)


def _layer_norm(y, g, b):
    mu = jnp.mean(y, axis=-1, keepdims=True)
    yc = y - mu
    var = jnp.mean(yc * yc, axis=-1, keepdims=True)
    return yc * lax.rsqrt(var + LN_EPS) * g + b


def _silu(a):
    return a * jax.nn.sigmoid(a)


def _slab_spec(rows, d, index_map, **kw):
    return pl.BlockSpec((d // LANES, rows, LANES), lambda *i: (0, index_map(*i), 0), **kw)


def _load_slabs(ref, rows=slice(None)):
    return jnp.concatenate([ref[c, rows, :] for c in range(ref.shape[0])], axis=1)


def _store_slabs(ref, val, rows=slice(None)):
    for c in range(ref.shape[0]):
        ref[c, rows, :] = val[:, c * LANES:(c + 1) * LANES]


def _dot(a, b):
    return jnp.dot(a, b, preferred_element_type=F32)


def _dot_nt(a, b):
    return lax.dot_general(a, b, (((1,), (1,)), ((), ())), preferred_element_type=F32)


def _dot_tn(a, b):
    return lax.dot_general(a, b, (((0,), (0,)), ((), ())), preferred_element_type=F32)


def _ffn_core(x, win_ref, wout_ref, d_ff):
    xb = x.astype(BF16)
    acc = jnp.zeros(x.shape, F32)
    for c in range(d_ff // FF_CHUNK):
        lo = c * FF_CHUNK
        a = _dot(xb, win_ref[:, lo:lo + FF_CHUNK])
        u = _dot(xb, win_ref[:, d_ff + lo:d_ff + lo + FF_CHUNK])
        acc = acc + _dot((_silu(a) * u).astype(BF16), wout_ref[lo:lo + FF_CHUNK, :])
    return acc


def _ffn_ln_kernel(x_ref, win_ref, wout_ref, g_ref, b_ref, o_ref, *, d_ff, alpha):
    x = x_ref[...]
    y = alpha * x + 0.5 * _ffn_core(x, win_ref, wout_ref, d_ff)
    _store_slabs(o_ref, _layer_norm(y, g_ref[...], b_ref[...]))


def _ffn_ln_ple_kernel(x_ref, p_ref, win_ref, wout_ref, g_ref, b_ref, wpg_ref, wpp_ref, o_ref, *, d_ff, alpha):
    x = x_ref[...]
    y = alpha * x + 0.5 * _ffn_core(x, win_ref, wout_ref, d_ff)
    h = _layer_norm(y, g_ref[...], b_ref[...])
    gate = jax.nn.sigmoid(_dot(h.astype(BF16), wpg_ref[...]))
    emb = _dot(p_ref[...].astype(BF16), wpp_ref[...])
    o_ref[...] = h + gate * emb


def _ffn_ln(x2d, w_in, w_out, g, b, alpha, ple=None):
    t, d = x2d.shape
    d_ff = w_out.shape[0]
    tm = min(TOKEN_TILE, t)
    row = lambda i: (i, 0)
    in_specs = [pl.BlockSpec((tm, d), row)]
    args = [x2d]
    if ple is not None:
        p2d, w_pg, w_pp = ple
        in_specs.append(pl.BlockSpec((tm, p2d.shape[1]), row))
        args.append(p2d)
    in_specs += [_resident(w_in.shape), _resident(w_out.shape), _resident(g.shape), _resident(b.shape)]
    args += [w_in, w_out, g, b]
    if ple is None:
        body = functools.partial(_ffn_ln_kernel, d_ff=d_ff, alpha=alpha)
    else:
        in_specs += [_resident(w_pg.shape), _resident(w_pp.shape)]
        args += [w_pg, w_pp]
        body = functools.partial(_ffn_ln_ple_kernel, d_ff=d_ff, alpha=alpha)
    return pl.pallas_call(
        body,
        grid=(t // tm,),
        in_specs=in_specs,
        out_specs=_slab_spec(tm, d, lambda i: i) if ple is None else pl.BlockSpec((tm, d), row),
        out_shape=jax.ShapeDtypeStruct((d // LANES, t, LANES) if ple is None else (t, d), F32),
        compiler_params=_params("parallel"),
        name="ffn_ln" if ple is None else "ffn_ln_ple",
    )(*args)


def _pair_lane_order():
    hd, half = ATT_HEAD_DIM, ROPE_DIM // 2
    a, b = list(range(hd)), list(range(hd, 2 * hd))
    order = (a[:half] + b[:half] + a[ROPE_DIM:] + a[half:ROPE_DIM] + b[half:ROPE_DIM] + b[ROPE_DIM:])
    return order, [src < hd for src in order]


def _rope_tables(pos_row_f32):
    half = ROPE_DIM // 2
    assert half == SUBLANES
    freq = lax.broadcasted_iota(jnp.int32, (half, 1), 0).astype(F32)
    ang = jnp.exp(-math.log(ROPE_THETA) * freq * (2.0 / ROPE_DIM)) * pos_row_f32
    k_row = lax.broadcasted_iota(jnp.int32, (4 * half, LANES), 0)
    k_lane = lax.broadcasted_iota(jnp.int32, (4 * half, LANES), 1)
    take = jnp.logical_and((k_row & (half - 1)) == (k_lane & (half - 1)), k_row < 3 * half)
    spread_matrix = jnp.where(take, 1.0, 0.0).astype(BF16)

    def spread(c8):
        hi = c8.astype(BF16).astype(F32)
        mid = (c8 - hi).astype(BF16).astype(F32)
        lo = c8 - hi - mid
        terms = jnp.concatenate([hi, mid, lo, jnp.zeros_like(hi)], axis=0).astype(BF16)
        return _dot_tn(terms, spread_matrix)

    lane = lax.broadcasted_iota(jnp.int32, (1, LANES), 1)
    rotary = (lane & (ATT_HEAD_DIM - 1)) < ROPE_DIM
    sign = jnp.where(lane < ATT_HEAD_DIM, -1.0, 1.0)
    return jnp.where(rotary, spread(jnp.cos(ang)), 1.0), jnp.where(rotary, spread(jnp.sin(ang)) * sign, 0.0)


def _merge_lse(o_a, lse_a, o_b, lse_b, want_lse):
    m = jnp.maximum(lse_a, lse_b)
    w_a = jnp.exp(lse_a - m)
    w_b = jnp.exp(lse_b - m)
    den = w_a + w_b
    o = (w_a * o_a + w_b * o_b) / den
    return o, (m + jnp.log(den) if want_lse else None)


def _attn_fused_kernel(x_ref, pos_ref, w_ref, o_ref, cos_s, sin_s, xg, cos_g, sin_g, q_buf, kv_ext,
                       carry_near, carry_mid, carry_far, step_o, step_lse, acc_o, acc_lse):
    n = pl.program_id(1)
    blk, gw = ATT_BLOCK, ATT_GROUP_WIDTH
    (_, d_near), (_, d_mid), (_, d_far) = ATT_GROUPS
    region = ATT_TILE // d_mid
    assert d_near == 1 and d_far == d_mid * d_mid and region == ATT_STEP_ROWS and d_mid == ATT_STEP_BLOCKS

    @pl.when(n == 0)
    def _():
        carry_near[...] = jnp.zeros_like(carry_near)
        carry_mid[...] = jnp.zeros_like(carry_mid)
        carry_far[...] = jnp.zeros_like(carry_far)

    cos_s[...], sin_s[...] = _rope_tables(pos_ref[...].astype(F32))

    row = lax.broadcasted_iota(jnp.int32, (blk, blk), 0)
    col = lax.broadcasted_iota(jnp.int32, (blk, blk), 1)
    neg_inf = jnp.float32(-jnp.inf)
    bias_own = jnp.where(col <= row, 0.0, neg_inf)
    bias_prev = jnp.where(col >= row, 0.0, neg_inf)
    lane = lax.broadcasted_iota(jnp.int32, (1, LANES), 1)
    first = lane < ATT_HEAD_DIM
    lane_j = lane & (ATT_HEAD_DIM - 1)
    half = ROPE_DIM // 2
    first_qk = jnp.logical_or(lane_j < half, jnp.logical_and(lane < ATT_HEAD_DIM, lane_j >= ROPE_DIM))
    ones_a = jnp.broadcast_to(jnp.where(first, 1.0, 0.0).astype(BF16), (blk, LANES))
    ones_b = jnp.broadcast_to(jnp.where(first, 0.0, 1.0).astype(BF16), (blk, LANES))
    rep = gw // LANES

    def project(group, xb, cos, sin):
        for part in range(3):
            z = _dot(xb, w_ref[group, :, part * gw:(part + 1) * gw])
            if part < 2:
                slabs = []
                for c in range(rep):
                    zc = z[:, c * LANES:(c + 1) * LANES]
                    slabs.append(zc * cos + pltpu.roll(zc, ATT_HEAD_DIM, 1) * sin)
                z = jnp.concatenate(slabs, axis=1)
            if part == 0:
                q_buf[...] = (z * ATT_HEAD_DIM ** -0.5).astype(BF16)
            else:
                kv_ext[blk:blk + ATT_STEP_ROWS, (part - 1) * gw:part * gw] = z.astype(BF16)

    def run_blocks(prev_ref, prev_base, prev_valid):
        def scores(a):
            off = pl.multiple_of(a * blk, blk)
            q = q_buf[pl.ds(off, blk), :]
            own = kv_ext[pl.ds(off + blk, blk), 0:gw]
            prev = prev_ref[pl.ds(prev_base + off, blk), 0:gw]
            bias = jnp.concatenate([jnp.where(prev_valid(a), bias_prev, neg_inf), bias_own], axis=1)
            zero = jnp.zeros((blk, LANES), BF16)
            e_all, m_all = [], []
            for p in range(rep):
                sl = slice(p * LANES, (p + 1) * LANES)
                q2 = q[:, sl]
                q_cat = jnp.concatenate([jnp.where(first_qk, q2, zero), jnp.where(first_qk, zero, q2)], axis=0)
                k_cat = jnp.concatenate([prev[:, sl], own[:, sl]], axis=0)
                s = _dot_nt(q_cat, k_cat)
                e_parts, m_parts = [], []
                for hd in range(2):
                    s_h = s[hd * blk:(hd + 1) * blk, :] + bias
                    m = jnp.max(s_h, axis=1, keepdims=True)
                    e_parts.append(jnp.exp(s_h - m).astype(BF16))
                    m_parts.append(m)
                e_all.append(jnp.concatenate(e_parts, axis=1))
                m_all.append(jnp.where(first, m_parts[0], m_parts[1]))
            return e_all, m_all

        def values(a, e_all, m_all):
            off = pl.multiple_of(a * blk, blk)
            own = kv_ext[pl.ds(off + blk, blk), gw:2 * gw]
            prev = prev_ref[pl.ds(prev_base + off, blk), gw:2 * gw]
            zero = jnp.zeros((blk, LANES), BF16)
            for p in range(rep):
                sl = slice(p * LANES, (p + 1) * LANES)
                vp2, vo2 = prev[:, sl], own[:, sl]
                v_cat = jnp.concatenate([
                    jnp.concatenate([jnp.where(first, vp2, zero), ones_a], axis=1),
                    jnp.concatenate([jnp.where(first, vo2, zero), ones_a], axis=1),
                    jnp.concatenate([jnp.where(first, zero, vp2), ones_b], axis=1),
                    jnp.concatenate([jnp.where(first, zero, vo2), ones_b], axis=1)], axis=0)
                r = _dot(e_all[p], v_cat)
                den = r[:, LANES:]
                step_o[p, pl.ds(off, blk), :] = r[:, :LANES] / den
                step_lse[p, pl.ds(off, blk), :] = m_all[p] + jnp.log(den)

        def body(i, carry):
            blocks = [ATT_BLOCKS_PER_TRIP * i + u for u in range(ATT_BLOCKS_PER_TRIP)]
            staged = [scores(a) for a in blocks]
            for a, (e_all, m_all) in zip(blocks, staged):
                values(a, e_all, m_all)
            return carry

        lax.fori_loop(0, ATT_STEP_BLOCKS // ATT_BLOCKS_PER_TRIP, body, 0)

    def rows_of(ref3, rows):
        return jnp.concatenate([ref3[c, rows, :] for c in range(ref3.shape[0])], axis=1)

    seen_tile = n > 0
    chained = lambda a: jnp.logical_or(a > 0, seen_tile)

    kv_ext[0:blk, :] = carry_near[...]
    for j in range(ATT_TILE // ATT_STEP_ROWS):
        rows = slice(j * ATT_STEP_ROWS, (j + 1) * ATT_STEP_ROWS)
        project(0, rows_of(x_ref, rows).astype(BF16), cos_s[rows, :], sin_s[rows, :])
        run_blocks(kv_ext, 0, (lambda a: True) if j > 0 else chained)
        kv_ext[0:blk, :] = kv_ext[ATT_STEP_ROWS:ATT_STEP_ROWS + blk, :]
        for rho in range(d_mid):
            src = pl.ds(rho, blk, stride=d_mid)
            dst = slice(rho * region + j * blk, rho * region + (j + 1) * blk)
            for c in range(rep):
                acc_o[c, dst, :] = step_o[c, src, :]
                acc_lse[c, dst, :] = step_lse[c, src, :]
    carry_near[...] = kv_ext[0:blk, :]

    for rho in range(d_mid):
        src = pl.ds(rho, region, stride=d_mid)
        reg = slice(rho * region, (rho + 1) * region)
        for c in range(xg.shape[0]):
            xg[c] = x_ref[c, src, :]
        cos_g[...] = cos_s[src, :]
        sin_g[...] = sin_s[src, :]

        kv_ext[0:blk, :] = carry_mid[rho * blk:(rho + 1) * blk, :]
        project(1, rows_of(xg, slice(None)).astype(BF16), cos_g[...], sin_g[...])
        run_blocks(kv_ext, 0, chained)
        carry_mid[rho * blk:(rho + 1) * blk, :] = kv_ext[ATT_STEP_ROWS:ATT_STEP_ROWS + blk, :]
        for c in range(rep):
            o, lse = _merge_lse(acc_o[c, reg, :], acc_lse[c, reg, :], step_o[c], step_lse[c], True)
            acc_o[c, reg, :] = o
            acc_lse[c, reg, :] = lse

        far = [pl.ds(a, blk, stride=d_mid) for a in range(ATT_STEP_BLOCKS)]
        project(2, jnp.concatenate([rows_of(xg, f) for f in far], axis=0).astype(BF16),
                jnp.concatenate([cos_g[f, :] for f in far], axis=0), jnp.concatenate([sin_g[f, :] for f in far], axis=0))
        run_blocks(carry_far, rho * ATT_STEP_ROWS, lambda a: seen_tile)
        carry_far[rho * ATT_STEP_ROWS:(rho + 1) * ATT_STEP_ROWS, :] = kv_ext[blk:blk + ATT_STEP_ROWS, :]
        for a in range(ATT_STEP_BLOCKS):
            dst = pl.ds(rho * region + a, blk, stride=d_mid)
            blk_rows = slice(a * blk, (a + 1) * blk)
            for c in range(rep):
                o, _ = _merge_lse(acc_o[c, dst, :], acc_lse[c, dst, :], step_o[c, blk_rows, :], step_lse[c, blk_rows, :], False)
                acc_o[c, dst, :] = o

        for c in range(rep):
            o_ref[c, src, :] = acc_o[c, reg, :]


def _attention(h_slabs, pos_rows, w_qkv, batch, seq):
    t, d = h_slabs.shape[1], w_qkv.shape[1]
    gw = ATT_GROUP_WIDTH
    assert seq % ATT_TILE == 0
    tiles = seq // ATT_TILE
    tile = lambda b, n: b * tiles + n
    kv_rows = ATT_BLOCK + ATT_STEP_ROWS
    d_mid, d_far = ATT_GROUPS[1][1], ATT_GROUPS[2][1]
    return pl.pallas_call(
        _attn_fused_kernel,
        grid=(batch, tiles),
        in_specs=[
            _slab_spec(ATT_TILE, d, tile),
            pl.BlockSpec((None, 1, ATT_TILE), lambda b, n: (tile(b, n), 0, 0)),
            _resident(w_qkv.shape),
        ],
        out_specs=_slab_spec(ATT_TILE, gw, tile, pipeline_mode=pl.Buffered(1)),
        out_shape=jax.ShapeDtypeStruct((gw // LANES, t, LANES), F32),
        scratch_shapes=[
            pltpu.VMEM((ATT_TILE, LANES), F32),
            pltpu.VMEM((ATT_TILE, LANES), F32),
            pltpu.VMEM((d // LANES, ATT_STEP_ROWS, LANES), F32),
            pltpu.VMEM((ATT_STEP_ROWS, LANES), F32),
            pltpu.VMEM((ATT_STEP_ROWS, LANES), F32),
            pltpu.VMEM((ATT_STEP_ROWS, gw), BF16),
            pltpu.VMEM((kv_rows, 2 * gw), BF16),
            pltpu.VMEM((ATT_BLOCK, 2 * gw), BF16),
            pltpu.VMEM((d_mid * ATT_BLOCK, 2 * gw), BF16),
            pltpu.VMEM((d_far * ATT_BLOCK, 2 * gw), BF16),
            pltpu.VMEM((gw // LANES, ATT_STEP_ROWS, LANES), F32),
            pltpu.VMEM((gw // LANES, ATT_STEP_ROWS, LANES), F32),
            pltpu.VMEM((gw // LANES, ATT_TILE, LANES), F32),
            pltpu.VMEM((gw // LANES, ATT_TILE, LANES), F32),
        ],
        compiler_params=_params("parallel", "arbitrary"),
        name="attn_fused",
    )(h_slabs, pos_rows, w_qkv)


def _split2(x):
    hi = x.astype(BF16)
    lo = (x - hi.astype(F32)).astype(BF16)
    return hi, lo


def _rows_bcast(x, row_in_block):
    parts = [jnp.broadcast_to(x[i + row_in_block:i + row_in_block + 1, :], (SUBLANES, x.shape[1]))
             for i in range(0, x.shape[0], SUBLANES)]
    return jnp.concatenate(parts, axis=0)


def _hgrn_kernel(lbl_ref, ng_ref, x_ref, w_ref, o_ref, z_s, *st_refs, layer, n_chunks):
    c = HGRN_CHUNK
    width = HGRN_HEADS * HGRN_DIM
    blk = SUBLANES
    nblk = c // blk
    kinds = {"q": 0, "f": 1, "i": 2, "g": 3}

    @pl.when(pl.program_id(1) == 0)
    def _():
        for st_ref in st_refs:
            st_ref[...] = jnp.zeros_like(st_ref)

    lgt = lbl_ref[...]
    ex = jnp.exp(lgt - jnp.max(lgt, axis=0, keepdims=True))
    sm = ex / jnp.sum(ex, axis=0, keepdims=True)
    lb = jnp.sum(sm[0:layer + 1, :], axis=0, keepdims=True)

    r_i = lax.broadcasted_iota(jnp.int32, (c, c), 0)
    c_i = lax.broadcasted_iota(jnp.int32, (c, c), 1)
    tri = (c_i <= r_i).astype(BF16)
    rowid = lax.broadcasted_iota(jnp.int32, (c, 1), 0)
    second = {hs: (rowid & hs) != 0 for hs in (1, 2, 4)}
    pair = {hs: jnp.logical_and((r_i // (2 * hs)) == (c_i // (2 * hs)),
                                jnp.logical_and((r_i & hs) != 0, (c_i & hs) == 0)) for hs in (1, 2, 4)}
    diag = r_i == c_i
    ng = ng_ref[...]
    zeros_blk = jnp.zeros((blk, HGRN_DIM), F32)
    heads = [slice(h * HGRN_DIM, (h + 1) * HGRN_DIM) for h in range(HGRN_HEADS)]

    xb = _load_slabs(x_ref).astype(BF16)

    def project(kind):
        lo = kinds[kind] * width
        for j in range(width // PROJ_CHUNK):
            sl = slice(lo + j * PROJ_CHUNK, lo + (j + 1) * PROJ_CHUNK)
            z_s[:, sl] = _dot(xb, w_ref[:, sl])

    def pre(kind, ci):
        lo = kinds[kind] * width
        return z_s[ci * c:(ci + 1) * c, lo:lo + width]

    def decay_part(ci):
        f = lb + (1.0 - lb) * jax.nn.sigmoid(pre("f", ci))
        hi, lo = _split2(jnp.log(f))
        b = _dot(tri, hi) + _dot(tri, lo)
        return f, b

    def chunk(ci, f, b):
        rows = slice(ci * c, (ci + 1) * c)
        k = 1.0 - f
        q = _silu(pre("q", ci))
        v = pre("i", ci)
        b_last = b[c - 1:c, :]
        qs = (q * jnp.exp(b)).astype(BF16)
        ks = (k * jnp.exp(b_last - b)).astype(BF16)
        decay = jnp.exp(b_last)
        qb, kb, vb = q.astype(BF16), k.astype(BF16), v.astype(BF16)

        q_far, k_far = [], []
        for j in range(nblk - 1):
            lo_r = (j + 1) * blk
            b_end = b[lo_r - 1:lo_r, :]
            q_far.append(q[lo_r:, :] * jnp.exp(b[lo_r:, :] - b_end))
            k_far.append(k[lo_r - blk:lo_r, :] * jnp.exp(b_end - b[lo_r - blk:lo_r, :]))

        lvl_q, lvl_k = {}, {}
        for hs, bref in ((4, _rows_bcast(b, 3)),
                         (2, jnp.where(second[4], _rows_bcast(b, 5), _rows_bcast(b, 1)))):
            x = jnp.exp(-jnp.abs(b - bref))
            lvl_q[hs] = jnp.where(second[hs], q * x, 0.0).astype(BF16)
            lvl_k[hs] = jnp.where(second[hs], 0.0, k * x).astype(BF16)
        lvl_q[1] = jnp.where(second[1], q * f, 0.0).astype(BF16)
        lvl_k[1] = jnp.where(second[1], 0.0, k).astype(BF16)

        o_inter, att_all = [], []
        for h, sl in enumerate(heads):
            st = st_refs[h][...]
            o_inter.append(_dot_nt(qs[:, sl], st.astype(BF16)))
            q_cat = jnp.concatenate(
                [jnp.concatenate([jnp.zeros(((j + 1) * blk, HGRN_DIM), F32), q_far[j][:, sl]], axis=0)
                 for j in range(nblk - 1)], axis=1).astype(BF16)
            k_cat = jnp.concatenate(
                [jnp.concatenate([zeros_blk] * j + [k_far[j][:, sl]] + [zeros_blk] * (nblk - 1 - j), axis=0)
                 for j in range(nblk - 1)], axis=1).astype(BF16)
            att = _dot_nt(q_cat, k_cat)
            for hs in (4, 2, 1):
                att = att + jnp.where(pair[hs], _dot_nt(lvl_q[hs][:, sl], lvl_k[hs][:, sl]), 0.0)
            att = att + jnp.where(diag, _dot_nt(qb[:, sl], kb[:, sl]), 0.0)
            att_all.append(att.astype(BF16))
            st_refs[h][...] = st * decay[:, sl] + _dot_tn(vb[:, sl], ks[:, sl])
        g_act = _silu(pre("g", ci))
        for h, sl in enumerate(heads):
            o = o_inter[h] + _dot(att_all[h], vb[:, sl])
            o = o * lax.rsqrt(jnp.mean(o * o, axis=1, keepdims=True) + RMS_EPS)
            o_ref[rows, sl] = (o * ng[:, sl] * g_act[:, sl]).astype(o_ref.dtype)

    project("f")
    decays = [decay_part(ci) for ci in range(n_chunks)]
    project("q")
    project("i")
    project("g")
    for ci, (f, b) in enumerate(decays):
        chunk(ci, f, b)


def _hgrn_scan(h_slabs, w_hgrn, lb_logits, norm_g, layer, batch, seq):
    width = HGRN_HEADS * HGRN_DIM
    d = w_hgrn.shape[0]
    ts = min(HGRN_STEP_TOKENS, seq)
    steps = seq // ts
    body = functools.partial(_hgrn_kernel, layer=layer, n_chunks=ts // HGRN_CHUNK)
    return pl.pallas_call(
        body,
        grid=(batch, steps),
        in_specs=[
            _resident(lb_logits.shape),
            _resident(norm_g.shape),
            _slab_spec(ts, d, lambda b, s: b * steps + s),
            _resident(w_hgrn.shape),
        ],
        out_specs=pl.BlockSpec((ts, width), lambda b, s: (b * steps + s, 0)),
        out_shape=jax.ShapeDtypeStruct((batch * seq, width), BF16),
        scratch_shapes=[pltpu.VMEM((ts, w_hgrn.shape[1]), F32)] + [pltpu.VMEM((HGRN_DIM, HGRN_DIM), F32)] * HGRN_HEADS,
        compiler_params=_params("parallel", "arbitrary"),
        name="hgrn_scan",
    )(lb_logits, norm_g, h_slabs, w_hgrn)


def _merge_kernel(h_ref, oa_ref, oh_ref, wg_ref, wpa_ref, wpb_ref, wo_ref, g_ref, b_ref, out_ref, *, alpha):
    h = _load_slabs(h_ref)
    d = h.shape[1]
    hb = h.astype(BF16)
    y_att = _dot(_load_slabs(oa_ref).astype(BF16), wpa_ref[...])
    y_hgrn = _dot(oh_ref[...], wpb_ref[...])
    merged = (jax.nn.sigmoid(_dot(hb, wg_ref[:, 0:d])) * y_att + jax.nn.sigmoid(_dot(hb, wg_ref[:, d:2 * d])) * y_hgrn)
    mix = _dot(merged.astype(BF16), wo_ref[...])
    out_ref[...] = _layer_norm(alpha * h + mix, g_ref[...], b_ref[...])


def _merge(h_slabs, o_att, o_hgrn, w_gate, w_pa, w_pb, w_o, g, b, alpha):
    t, d = o_hgrn.shape
    tm = min(TOKEN_TILE, t)
    row = lambda i: (i, 0)
    in_specs = [_slab_spec(tm, d, lambda i: i),
                _slab_spec(tm, o_att.shape[0] * LANES, lambda i: i),
                pl.BlockSpec((tm, d), row)]
    in_specs += [_resident(a.shape) for a in (w_gate, w_pa, w_pb, w_o, g, b)]
    return pl.pallas_call(
        functools.partial(_merge_kernel, alpha=alpha),
        grid=(t // tm,),
        in_specs=in_specs,
        out_specs=pl.BlockSpec((tm, d), row),
        out_shape=jax.ShapeDtypeStruct((t, d), F32),
        compiler_params=_params("parallel"),
        name="merge_ln",
    )(h_slabs, o_att, o_hgrn, w_gate, w_pa, w_pb, w_o, g, b)


def kernel(x, p, positions, ffn1_w_in, ffn1_w_out, ln1_g, ln1_b, w_mix_in, hgrn_lb_logits, hgrn_norm_g, w_proj_attn, w_proj_hgrn, w_mix_out, ln2_g, ln2_b, ffn2_w_in, ffn2_w_out, ln3_g, ln3_b, ple_w_proj, ple_w_gate):
    batch, seq, d_model = x.shape
    depth = ffn1_w_in.shape[0]
    alpha = (2 * depth) ** 0.25
    width = HGRN_HEADS * HGRN_DIM
    gw = ATT_GROUP_WIDTH
    t = batch * seq
    h = x.reshape(t, d_model)
    pos_rows = positions.reshape(t // ATT_TILE, 1, ATT_TILE)
    lb_logits = hgrn_lb_logits.astype(F32)
    vec = lambda a: a.reshape(1, -1).astype(F32)
    c_hgrn = 3 * ATT_WIDTH
    c_gate = c_hgrn + 4 * width
    pair_order, _ = _pair_lane_order()
    runs = []
    for src in pair_order:
        if runs and runs[-1][1] == src:
            runs[-1][1] = src + 1
        else:
            runs.append([src, src + 1])
    for i in range(depth):
        w_in = w_mix_in[i]
        def group_cols(part, g):
            lo = part * ATT_WIDTH + g * gw
            if part == 2:
                return [w_in[:, lo:lo + gw]]
            return [w_in[:, lo + p * LANES + a:lo + p * LANES + b] for p in range(gw // LANES) for a, b in runs]
        w_qkv = jnp.stack([jnp.concatenate([piece for part in range(3) for piece in group_cols(part, g)], axis=1)
                           for g in range(len(ATT_GROUPS))]).astype(BF16)
        w_hgrn = w_in[:, c_hgrn:c_gate].astype(BF16)
        w_gate = w_in[:, c_gate:].astype(BF16)

        h = _ffn_ln(h, ffn1_w_in[i].astype(BF16), ffn1_w_out[i].astype(BF16), vec(ln1_g[i]), vec(ln1_b[i]), alpha)
        for window, dilation in ATT_GROUPS:
            assert window == ATT_BLOCK * dilation
        o_att = _attention(h, pos_rows, w_qkv, batch, seq)
        o_hgrn = _hgrn_scan(h, w_hgrn, lb_logits, vec(hgrn_norm_g[i]), i, batch, seq)
        h = _merge(h, o_att, o_hgrn, w_gate, w_proj_attn[i].astype(BF16), w_proj_hgrn[i].astype(BF16),
                   w_mix_out[i].astype(BF16), vec(ln2_g[i]), vec(ln2_b[i]), alpha)
        h = _ffn_ln(h, ffn2_w_in[i].astype(BF16), ffn2_w_out[i].astype(BF16), vec(ln3_g[i]), vec(ln3_b[i]), alpha,
                    ple=(p[i].reshape(t, -1), ple_w_gate[i].astype(BF16), ple_w_proj[i].astype(BF16)))
    return h.reshape(batch, seq, d_model)
```

```python
import functools
import math

import jax
import jax.numpy as jnp
from jax import lax
from jax.experimental import pallas as pl
from jax.experimental.pallas import tpu as pltpu

F32 = jnp.float32
BF16 = jnp.bfloat16

ATT_HEAD_DIM = 64
ATT_HEADS_PER_GROUP = 8
ATT_GROUPS = ((128, 1), (512, 4), (2048, 16))
ATT_GROUP_WIDTH = ATT_HEADS_PER_GROUP * ATT_HEAD_DIM
ATT_WIDTH = ATT_GROUP_WIDTH * len(ATT_GROUPS)
ATT_BLOCK = 128
ROPE_THETA = 500000.0
ROPE_DIM = ATT_HEAD_DIM // 4
HGRN_HEADS = 8
HGRN_DIM = 128
LN_EPS = 1e-5
RMS_EPS = 1e-6

LANES = 128
SUBLANES = 8
VMEM_LIMIT_BYTES = 62 * 1024 * 1024

TOKEN_TILE = 1024
FF_CHUNK = 256
PROJ_CHUNK = 512
ATT_STEP_BLOCKS = 4
ATT_STEP_ROWS = ATT_STEP_BLOCKS * ATT_BLOCK
ATT_BLOCKS_PER_TRIP = 2
ATT_TILE = ATT_BLOCK * max(d for _, d in ATT_GROUPS)
HGRN_CHUNK = 64
HGRN_STEP_TOKENS = 512


def _params(*sem):
    return pltpu.CompilerParams(dimension_semantics=sem, vmem_limit_bytes=VMEM_LIMIT_BYTES)


def _resident(shape):
    nd = len(shape)
    return pl.BlockSpec(shape, lambda *_: (0,) * nd)


def _layer_norm(y, g, b):
    mu = jnp.mean(y, axis=-1, keepdims=True)
    yc = y - mu
    var = jnp.mean(yc * yc, axis=-1, keepdims=True)
    return yc * lax.rsqrt(var + LN_EPS) * g + b


def _silu(a):
    return a * jax.nn.sigmoid(a)


def _slab_spec(rows, d, index_map, **kw):
    return pl.BlockSpec((d // LANES, rows, LANES), lambda *i: (0, index_map(*i), 0), **kw)


def _load_slabs(ref, rows=slice(None)):
    return jnp.concatenate([ref[c, rows, :] for c in range(ref.shape[0])], axis=1)


def _store_slabs(ref, val, rows=slice(None)):
    for c in range(ref.shape[0]):
        ref[c, rows, :] = val[:, c * LANES:(c + 1) * LANES]


def _dot(a, b):
    return jnp.dot(a, b, preferred_element_type=F32)


def _dot_nt(a, b):
    return lax.dot_general(a, b, (((1,), (1,)), ((), ())), preferred_element_type=F32)


def _dot_tn(a, b):
    return lax.dot_general(a, b, (((0,), (0,)), ((), ())), preferred_element_type=F32)


def _ffn_core(x, win_ref, wout_ref, d_ff):
    xb = x.astype(BF16)
    acc = jnp.zeros(x.shape, F32)
    for c in range(d_ff // FF_CHUNK):
        lo = c * FF_CHUNK
        a = _dot(xb, win_ref[:, lo:lo + FF_CHUNK])
        u = _dot(xb, win_ref[:, d_ff + lo:d_ff + lo + FF_CHUNK])
        acc = acc + _dot((_silu(a) * u).astype(BF16), wout_ref[lo:lo + FF_CHUNK, :])
    return acc


def _ffn_ln_kernel(x_ref, win_ref, wout_ref, g_ref, b_ref, o_ref, *, d_ff, alpha):
    x = x_ref[...]
    y = alpha * x + 0.5 * _ffn_core(x, win_ref, wout_ref, d_ff)
    _store_slabs(o_ref, _layer_norm(y, g_ref[...], b_ref[...]))


def _ffn_ln_ple_kernel(x_ref, p_ref, win_ref, wout_ref, g_ref, b_ref, wpg_ref, wpp_ref, o_ref, *, d_ff, alpha):
    x = x_ref[...]
    y = alpha * x + 0.5 * _ffn_core(x, win_ref, wout_ref, d_ff)
    h = _layer_norm(y, g_ref[...], b_ref[...])
    gate = jax.nn.sigmoid(_dot(h.astype(BF16), wpg_ref[...]))
    emb = _dot(p_ref[...].astype(BF16), wpp_ref[...])
    o_ref[...] = h + gate * emb


def _ffn_ln(x2d, w_in, w_out, g, b, alpha, ple=None):
    t, d = x2d.shape
    d_ff = w_out.shape[0]
    tm = min(TOKEN_TILE, t)
    row = lambda i: (i, 0)
    in_specs = [pl.BlockSpec((tm, d), row)]
    args = [x2d]
    if ple is not None:
        p2d, w_pg, w_pp = ple
        in_specs.append(pl.BlockSpec((tm, p2d.shape[1]), row))
        args.append(p2d)
    in_specs += [_resident(w_in.shape), _resident(w_out.shape), _resident(g.shape), _resident(b.shape)]
    args += [w_in, w_out, g, b]
    if ple is None:
        body = functools.partial(_ffn_ln_kernel, d_ff=d_ff, alpha=alpha)
    else:
        in_specs += [_resident(w_pg.shape), _resident(w_pp.shape)]
        args += [w_pg, w_pp]
        body = functools.partial(_ffn_ln_ple_kernel, d_ff=d_ff, alpha=alpha)
    return pl.pallas_call(
        body,
        grid=(t // tm,),
        in_specs=in_specs,
        out_specs=_slab_spec(tm, d, lambda i: i) if ple is None else pl.BlockSpec((tm, d), row),
        out_shape=jax.ShapeDtypeStruct((d // LANES, t, LANES) if ple is None else (t, d), F32),
        compiler_params=_params("parallel"),
        name="ffn_ln" if ple is None else "ffn_ln_ple",
    )(*args)


def _pair_lane_order():
    hd, half = ATT_HEAD_DIM, ROPE_DIM // 2
    a, b = list(range(hd)), list(range(hd, 2 * hd))
    order = (a[:half] + b[:half] + a[ROPE_DIM:] + a[half:ROPE_DIM] + b[half:ROPE_DIM] + b[ROPE_DIM:])
    return order, [src < hd for src in order]


def _rope_tables(pos_row_f32):
    half = ROPE_DIM // 2
    assert half == SUBLANES
    freq = lax.broadcasted_iota(jnp.int32, (half, 1), 0).astype(F32)
    ang = jnp.exp(-math.log(ROPE_THETA) * freq * (2.0 / ROPE_DIM)) * pos_row_f32
    k_row = lax.broadcasted_iota(jnp.int32, (4 * half, LANES), 0)
    k_lane = lax.broadcasted_iota(jnp.int32, (4 * half, LANES), 1)
    take = jnp.logical_and((k_row & (half - 1)) == (k_lane & (half - 1)), k_row < 3 * half)
    spread_matrix = jnp.where(take, 1.0, 0.0).astype(BF16)

    def spread(c8):
        hi = c8.astype(BF16).astype(F32)
        mid = (c8 - hi).astype(BF16).astype(F32)
        lo = c8 - hi - mid
        terms = jnp.concatenate([hi, mid, lo, jnp.zeros_like(hi)], axis=0).astype(BF16)
        return _dot_tn(terms, spread_matrix)

    lane = lax.broadcasted_iota(jnp.int32, (1, LANES), 1)
    rotary = (lane & (ATT_HEAD_DIM - 1)) < ROPE_DIM
    sign = jnp.where(lane < ATT_HEAD_DIM, -1.0, 1.0)
    return jnp.where(rotary, spread(jnp.cos(ang)), 1.0), jnp.where(rotary, spread(jnp.sin(ang)) * sign, 0.0)


def _merge_lse(o_a, lse_a, o_b, lse_b, want_lse):
    m = jnp.maximum(lse_a, lse_b)
    w_a = jnp.exp(lse_a - m)
    w_b = jnp.exp(lse_b - m)
    den = w_a + w_b
    o = (w_a * o_a + w_b * o_b) / den
    return o, (m + jnp.log(den) if want_lse else None)


def _attn_fused_kernel(x_ref, pos_ref, w_ref, o_ref, cos_s, sin_s, xg, cos_g, sin_g, q_buf, kv_ext,
                       carry_near, carry_mid, carry_far, step_o, step_lse, acc_o, acc_lse):
    n = pl.program_id(1)
    blk, gw = ATT_BLOCK, ATT_GROUP_WIDTH
    (_, d_near), (_, d_mid), (_, d_far) = ATT_GROUPS
    region = ATT_TILE // d_mid
    assert d_near == 1 and d_far == d_mid * d_mid and region == ATT_STEP_ROWS and d_mid == ATT_STEP_BLOCKS

    @pl.when(n == 0)
    def _():
        carry_near[...] = jnp.zeros_like(carry_near)
        carry_mid[...] = jnp.zeros_like(carry_mid)
        carry_far[...] = jnp.zeros_like(carry_far)

    cos_s[...], sin_s[...] = _rope_tables(pos_ref[...].astype(F32))

    row = lax.broadcasted_iota(jnp.int32, (blk, blk), 0)
    col = lax.broadcasted_iota(jnp.int32, (blk, blk), 1)
    neg_inf = jnp.float32(-jnp.inf)
    bias_own = jnp.where(col <= row, 0.0, neg_inf)
    bias_prev = jnp.where(col >= row, 0.0, neg_inf)
    lane = lax.broadcasted_iota(jnp.int32, (1, LANES), 1)
    first = lane < ATT_HEAD_DIM
    lane_j = lane & (ATT_HEAD_DIM - 1)
    half = ROPE_DIM // 2
    first_qk = jnp.logical_or(lane_j < half, jnp.logical_and(lane < ATT_HEAD_DIM, lane_j >= ROPE_DIM))
    ones_a = jnp.broadcast_to(jnp.where(first, 1.0, 0.0).astype(BF16), (blk, LANES))
    ones_b = jnp.broadcast_to(jnp.where(first, 0.0, 1.0).astype(BF16), (blk, LANES))
    rep = gw // LANES

    def project(group, xb, cos, sin):
        for part in range(3):
            lo = (3 * group + part) * gw
            z = _dot(xb, w_ref[:, lo:lo + gw])
            if part < 2:
                slabs = []
                for c in range(rep):
                    zc = z[:, c * LANES:(c + 1) * LANES]
                    slabs.append(zc * cos + pltpu.roll(zc, ATT_HEAD_DIM, 1) * sin)
                z = jnp.concatenate(slabs, axis=1)
            if part == 0:
                q_buf[...] = (z * ATT_HEAD_DIM ** -0.5).astype(BF16)
            else:
                kv_ext[blk:blk + ATT_STEP_ROWS, (part - 1) * gw:part * gw] = z.astype(BF16)

    def run_blocks(prev_ref, prev_base, prev_valid):
        def scores(a):
            off = pl.multiple_of(a * blk, blk)
            q = q_buf[pl.ds(off, blk), :]
            own = kv_ext[pl.ds(off + blk, blk), 0:gw]
            prev = prev_ref[pl.ds(prev_base + off, blk), 0:gw]
            bias = jnp.concatenate([jnp.where(prev_valid(a), bias_prev, neg_inf), bias_own], axis=1)
            zero = jnp.zeros((blk, LANES), BF16)
            e_all, m_all = [], []
            for p in range(rep):
                sl = slice(p * LANES, (p + 1) * LANES)
                q2 = q[:, sl]
                q_cat = jnp.concatenate([jnp.where(first_qk, q2, zero), jnp.where(first_qk, zero, q2)], axis=0)
                k_cat = jnp.concatenate([prev[:, sl], own[:, sl]], axis=0)
                s = _dot_nt(q_cat, k_cat)
                e_parts, m_parts = [], []
                for hd in range(2):
                    s_h = s[hd * blk:(hd + 1) * blk, :] + bias
                    m = jnp.max(s_h, axis=1, keepdims=True)
                    e_parts.append(jnp.exp(s_h - m).astype(BF16))
                    m_parts.append(m)
                e_all.append(jnp.concatenate(e_parts, axis=1))
                m_all.append(jnp.where(first, m_parts[0], m_parts[1]))
            return e_all, m_all

        def values(a, e_all, m_all):
            off = pl.multiple_of(a * blk, blk)
            own = kv_ext[pl.ds(off + blk, blk), gw:2 * gw]
            prev = prev_ref[pl.ds(prev_base + off, blk), gw:2 * gw]
            zero = jnp.zeros((blk, LANES), BF16)
            for p in range(rep):
                sl = slice(p * LANES, (p + 1) * LANES)
                vp2, vo2 = prev[:, sl], own[:, sl]
                v_cat = jnp.concatenate([
                    jnp.concatenate([jnp.where(first, vp2, zero), ones_a], axis=1),
                    jnp.concatenate([jnp.where(first, vo2, zero), ones_a], axis=1),
                    jnp.concatenate([jnp.where(first, zero, vp2), ones_b], axis=1),
                    jnp.concatenate([jnp.where(first, zero, vo2), ones_b], axis=1)], axis=0)
                r = _dot(e_all[p], v_cat)
                den = r[:, LANES:]
                step_o[p, pl.ds(off, blk), :] = r[:, :LANES] / den
                step_lse[p, pl.ds(off, blk), :] = m_all[p] + jnp.log(den)

        def body(i, carry):
            blocks = [ATT_BLOCKS_PER_TRIP * i + u for u in range(ATT_BLOCKS_PER_TRIP)]
            staged = [scores(a) for a in blocks]
            for a, (e_all, m_all) in zip(blocks, staged):
                values(a, e_all, m_all)
            return carry

        lax.fori_loop(0, ATT_STEP_BLOCKS // ATT_BLOCKS_PER_TRIP, body, 0)

    def rows_of(ref3, rows):
        return jnp.concatenate([ref3[c, rows, :] for c in range(ref3.shape[0])], axis=1)

    seen_tile = n > 0
    chained = lambda a: jnp.logical_or(a > 0, seen_tile)

    kv_ext[0:blk, :] = carry_near[...]
    for j in range(ATT_TILE // ATT_STEP_ROWS):
        rows = slice(j * ATT_STEP_ROWS, (j + 1) * ATT_STEP_ROWS)
        project(0, rows_of(x_ref, rows).astype(BF16), cos_s[rows, :], sin_s[rows, :])
        run_blocks(kv_ext, 0, (lambda a: True) if j > 0 else chained)
        kv_ext[0:blk, :] = kv_ext[ATT_STEP_ROWS:ATT_STEP_ROWS + blk, :]
        for rho in range(d_mid):
            src = pl.ds(rho, blk, stride=d_mid)
            dst = slice(rho * region + j * blk, rho * region + (j + 1) * blk)
            for c in range(rep):
                acc_o[c, dst, :] = step_o[c, src, :]
                acc_lse[c, dst, :] = step_lse[c, src, :]
    carry_near[...] = kv_ext[0:blk, :]

    for rho in range(d_mid):
        src = pl.ds(rho, region, stride=d_mid)
        reg = slice(rho * region, (rho + 1) * region)
        for c in range(xg.shape[0]):
            xg[c] = x_ref[c, src, :]
        cos_g[...] = cos_s[src, :]
        sin_g[...] = sin_s[src, :]

        kv_ext[0:blk, :] = carry_mid[rho * blk:(rho + 1) * blk, :]
        project(1, rows_of(xg, slice(None)).astype(BF16), cos_g[...], sin_g[...])
        run_blocks(kv_ext, 0, chained)
        carry_mid[rho * blk:(rho + 1) * blk, :] = kv_ext[ATT_STEP_ROWS:ATT_STEP_ROWS + blk, :]
        for c in range(rep):
            o, lse = _merge_lse(acc_o[c, reg, :], acc_lse[c, reg, :], step_o[c], step_lse[c], True)
            acc_o[c, reg, :] = o
            acc_lse[c, reg, :] = lse

        far = [pl.ds(a, blk, stride=d_mid) for a in range(ATT_STEP_BLOCKS)]
        project(2, jnp.concatenate([rows_of(xg, f) for f in far], axis=0).astype(BF16),
                jnp.concatenate([cos_g[f, :] for f in far], axis=0), jnp.concatenate([sin_g[f, :] for f in far], axis=0))
        run_blocks(carry_far, rho * ATT_STEP_ROWS, lambda a: seen_tile)
        carry_far[rho * ATT_STEP_ROWS:(rho + 1) * ATT_STEP_ROWS, :] = kv_ext[blk:blk + ATT_STEP_ROWS, :]
        for a in range(ATT_STEP_BLOCKS):
            dst = pl.ds(rho * region + a, blk, stride=d_mid)
            blk_rows = slice(a * blk, (a + 1) * blk)
            for c in range(rep):
                o, _ = _merge_lse(acc_o[c, dst, :], acc_lse[c, dst, :], step_o[c, blk_rows, :], step_lse[c, blk_rows, :], False)
                acc_o[c, dst, :] = o

        for c in range(rep):
            o_ref[c, src, :] = acc_o[c, reg, :]


def _attention(h_slabs, pos_rows, w_qkv, batch, seq):
    t, d = h_slabs.shape[1], w_qkv.shape[0]
    gw = ATT_GROUP_WIDTH
    assert seq % ATT_TILE == 0
    tiles = seq // ATT_TILE
    tile = lambda b, n: b * tiles + n
    kv_rows = ATT_BLOCK + ATT_STEP_ROWS
    d_mid, d_far = ATT_GROUPS[1][1], ATT_GROUPS[2][1]
    return pl.pallas_call(
        _attn_fused_kernel,
        grid=(batch, tiles),
        in_specs=[
            _slab_spec(ATT_TILE, d, tile),
            pl.BlockSpec((None, 1, ATT_TILE), lambda b, n: (tile(b, n), 0, 0)),
            _resident(w_qkv.shape),
        ],
        out_specs=_slab_spec(ATT_TILE, gw, tile),
        out_shape=jax.ShapeDtypeStruct((gw // LANES, t, LANES), F32),
        scratch_shapes=[
            pltpu.VMEM((ATT_TILE, LANES), F32),
            pltpu.VMEM((ATT_TILE, LANES), F32),
            pltpu.VMEM((d // LANES, ATT_STEP_ROWS, LANES), F32),
            pltpu.VMEM((ATT_STEP_ROWS, LANES), F32),
            pltpu.VMEM((ATT_STEP_ROWS, LANES), F32),
            pltpu.VMEM((ATT_STEP_ROWS, gw), BF16),
            pltpu.VMEM((kv_rows, 2 * gw), BF16),
            pltpu.VMEM((ATT_BLOCK, 2 * gw), BF16),
            pltpu.VMEM((d_mid * ATT_BLOCK, 2 * gw), BF16),
            pltpu.VMEM((d_far * ATT_BLOCK, 2 * gw), BF16),
            pltpu.VMEM((gw // LANES, ATT_STEP_ROWS, LANES), F32),
            pltpu.VMEM((gw // LANES, ATT_STEP_ROWS, LANES), F32),
            pltpu.VMEM((gw // LANES, ATT_TILE, LANES), F32),
            pltpu.VMEM((gw // LANES, ATT_TILE, LANES), F32),
        ],
        compiler_params=_params("parallel", "arbitrary"),
        name="attn_fused",
    )(h_slabs, pos_rows, w_qkv)


def _split2(x):
    hi = x.astype(BF16)
    lo = (x - hi.astype(F32)).astype(BF16)
    return hi, lo


def _rows_bcast(x, row_in_block):
    parts = [jnp.broadcast_to(x[i + row_in_block:i + row_in_block + 1, :], (SUBLANES, x.shape[1]))
             for i in range(0, x.shape[0], SUBLANES)]
    return jnp.concatenate(parts, axis=0)


def _hgrn_kernel(lbl_ref, ng_ref, x_ref, w_ref, o_ref, z_s, *st_refs, layer, n_chunks):
    c = HGRN_CHUNK
    width = HGRN_HEADS * HGRN_DIM
    blk = SUBLANES
    nblk = c // blk
    kinds = {"q": 0, "f": 1, "i": 2, "g": 3}

    @pl.when(pl.program_id(1) == 0)
    def _():
        for st_ref in st_refs:
            st_ref[...] = jnp.zeros_like(st_ref)

    lgt = lbl_ref[...]
    ex = jnp.exp(lgt - jnp.max(lgt, axis=0, keepdims=True))
    sm = ex / jnp.sum(ex, axis=0, keepdims=True)
    lb = jnp.sum(sm[0:layer + 1, :], axis=0, keepdims=True)

    r_i = lax.broadcasted_iota(jnp.int32, (c, c), 0)
    c_i = lax.broadcasted_iota(jnp.int32, (c, c), 1)
    tri = (c_i <= r_i).astype(BF16)
    rowid = lax.broadcasted_iota(jnp.int32, (c, 1), 0)
    second = {hs: (rowid & hs) != 0 for hs in (1, 2, 4)}
    pair = {hs: jnp.logical_and((r_i // (2 * hs)) == (c_i // (2 * hs)),
                                jnp.logical_and((r_i & hs) != 0, (c_i & hs) == 0)) for hs in (1, 2, 4)}
    diag = r_i == c_i
    ng = ng_ref[...]
    zeros_blk = jnp.zeros((blk, HGRN_DIM), F32)
    heads = [slice(h * HGRN_DIM, (h + 1) * HGRN_DIM) for h in range(HGRN_HEADS)]

    xb = _load_slabs(x_ref).astype(BF16)

    def project(kind):
        lo = kinds[kind] * width
        for j in range(width // PROJ_CHUNK):
            sl = slice(lo + j * PROJ_CHUNK, lo + (j + 1) * PROJ_CHUNK)
            z_s[:, sl] = _dot(xb, w_ref[:, sl])

    def pre(kind, ci):
        lo = kinds[kind] * width
        return z_s[ci * c:(ci + 1) * c, lo:lo + width]

    def decay_part(ci):
        f = lb + (1.0 - lb) * jax.nn.sigmoid(pre("f", ci))
        hi, lo = _split2(jnp.log(f))
        b = _dot(tri, hi) + _dot(tri, lo)
        return f, b

    def chunk(ci, f, b):
        rows = slice(ci * c, (ci + 1) * c)
        k = 1.0 - f
        q = _silu(pre("q", ci))
        v = pre("i", ci)
        b_last = b[c - 1:c, :]
        qs = (q * jnp.exp(b)).astype(BF16)
        ks = (k * jnp.exp(b_last - b)).astype(BF16)
        decay = jnp.exp(b_last)
        qb, kb, vb = q.astype(BF16), k.astype(BF16), v.astype(BF16)

        q_far, k_far = [], []
        for j in range(nblk - 1):
            lo_r = (j + 1) * blk
            b_end = b[lo_r - 1:lo_r, :]
            q_far.append(q[lo_r:, :] * jnp.exp(b[lo_r:, :] - b_end))
            k_far.append(k[lo_r - blk:lo_r, :] * jnp.exp(b_end - b[lo_r - blk:lo_r, :]))

        lvl_q, lvl_k = {}, {}
        for hs, bref in ((4, _rows_bcast(b, 3)),
                         (2, jnp.where(second[4], _rows_bcast(b, 5), _rows_bcast(b, 1)))):
            x = jnp.exp(-jnp.abs(b - bref))
            lvl_q[hs] = jnp.where(second[hs], q * x, 0.0).astype(BF16)
            lvl_k[hs] = jnp.where(second[hs], 0.0, k * x).astype(BF16)
        lvl_q[1] = jnp.where(second[1], q * f, 0.0).astype(BF16)
        lvl_k[1] = jnp.where(second[1], 0.0, k).astype(BF16)

        o_inter, att_all = [], []
        for h, sl in enumerate(heads):
            st = st_refs[h][...]
            o_inter.append(_dot_nt(qs[:, sl], st.astype(BF16)))
            q_cat = jnp.concatenate(
                [jnp.concatenate([jnp.zeros(((j + 1) * blk, HGRN_DIM), F32), q_far[j][:, sl]], axis=0)
                 for j in range(nblk - 1)], axis=1).astype(BF16)
            k_cat = jnp.concatenate(
                [jnp.concatenate([zeros_blk] * j + [k_far[j][:, sl]] + [zeros_blk] * (nblk - 1 - j), axis=0)
                 for j in range(nblk - 1)], axis=1).astype(BF16)
            att = _dot_nt(q_cat, k_cat)
            for hs in (4, 2, 1):
                att = att + jnp.where(pair[hs], _dot_nt(lvl_q[hs][:, sl], lvl_k[hs][:, sl]), 0.0)
            att = att + jnp.where(diag, _dot_nt(qb[:, sl], kb[:, sl]), 0.0)
            att_all.append(att.astype(BF16))
            st_refs[h][...] = st * decay[:, sl] + _dot_tn(vb[:, sl], ks[:, sl])
        g_act = _silu(pre("g", ci))
        for h, sl in enumerate(heads):
            o = o_inter[h] + _dot(att_all[h], vb[:, sl])
            o = o * lax.rsqrt(jnp.mean(o * o, axis=1, keepdims=True) + RMS_EPS)
            o_ref[rows, sl] = (o * ng[:, sl] * g_act[:, sl]).astype(o_ref.dtype)

    project("f")
    decays = [decay_part(ci) for ci in range(n_chunks)]
    project("q")
    project("i")
    project("g")
    for ci, (f, b) in enumerate(decays):
        chunk(ci, f, b)


def _hgrn_scan(h_slabs, w_hgrn, lb_logits, norm_g, layer, batch, seq):
    width = HGRN_HEADS * HGRN_DIM
    d = w_hgrn.shape[0]
    ts = min(HGRN_STEP_TOKENS, seq)
    steps = seq // ts
    body = functools.partial(_hgrn_kernel, layer=layer, n_chunks=ts // HGRN_CHUNK)
    return pl.pallas_call(
        body,
        grid=(batch, steps),
        in_specs=[
            _resident(lb_logits.shape),
            _resident(norm_g.shape),
            _slab_spec(ts, d, lambda b, s: b * steps + s),
            _resident(w_hgrn.shape),
        ],
        out_specs=pl.BlockSpec((ts, width), lambda b, s: (b * steps + s, 0)),
        out_shape=jax.ShapeDtypeStruct((batch * seq, width), BF16),
        scratch_shapes=[pltpu.VMEM((ts, w_hgrn.shape[1]), F32)] + [pltpu.VMEM((HGRN_DIM, HGRN_DIM), F32)] * HGRN_HEADS,
        compiler_params=_params("parallel", "arbitrary"),
        name="hgrn_scan",
    )(lb_logits, norm_g, h_slabs, w_hgrn)


def _merge_kernel(h_ref, oa_ref, oh_ref, wg_ref, wpa_ref, wpb_ref, wo_ref, g_ref, b_ref, out_ref, *, alpha):
    h = _load_slabs(h_ref)
    d = h.shape[1]
    hb = h.astype(BF16)
    y_att = _dot(_load_slabs(oa_ref).astype(BF16), wpa_ref[...])
    y_hgrn = _dot(oh_ref[...], wpb_ref[...])
    merged = (jax.nn.sigmoid(_dot(hb, wg_ref[:, 0:d])) * y_att + jax.nn.sigmoid(_dot(hb, wg_ref[:, d:2 * d])) * y_hgrn)
    mix = _dot(merged.astype(BF16), wo_ref[...])
    out_ref[...] = _layer_norm(alpha * h + mix, g_ref[...], b_ref[...])


def _merge(h_slabs, o_att, o_hgrn, w_gate, w_pa, w_pb, w_o, g, b, alpha):
    t, d = o_hgrn.shape
    tm = min(TOKEN_TILE, t)
    row = lambda i: (i, 0)
    in_specs = [_slab_spec(tm, d, lambda i: i),
                _slab_spec(tm, o_att.shape[0] * LANES, lambda i: i),
                pl.BlockSpec((tm, d), row)]
    in_specs += [_resident(a.shape) for a in (w_gate, w_pa, w_pb, w_o, g, b)]
    return pl.pallas_call(
        functools.partial(_merge_kernel, alpha=alpha),
        grid=(t // tm,),
        in_specs=in_specs,
        out_specs=pl.BlockSpec((tm, d), row),
        out_shape=jax.ShapeDtypeStruct((t, d), F32),
        compiler_params=_params("parallel"),
        name="merge_ln",
    )(h_slabs, o_att, o_hgrn, w_gate, w_pa, w_pb, w_o, g, b)


def kernel(x, p, positions, ffn1_w_in, ffn1_w_out, ln1_g, ln1_b, w_mix_in, hgrn_lb_logits, hgrn_norm_g, w_proj_attn, w_proj_hgrn, w_mix_out, ln2_g, ln2_b, ffn2_w_in, ffn2_w_out, ln3_g, ln3_b, ple_w_proj, ple_w_gate):
    batch, seq, d_model = x.shape
    depth = ffn1_w_in.shape[0]
    alpha = (2 * depth) ** 0.25
    width = HGRN_HEADS * HGRN_DIM
    gw = ATT_GROUP_WIDTH
    t = batch * seq
    h = x.reshape(t, d_model)
    pos_rows = positions.reshape(t // ATT_TILE, 1, ATT_TILE)
    lb_logits = hgrn_lb_logits.astype(F32)
    vec = lambda a: a.reshape(1, -1).astype(F32)
    c_hgrn = 3 * ATT_WIDTH
    c_gate = c_hgrn + 4 * width
    pair_order, _ = _pair_lane_order()
    att_cols = []
    for g in range(len(ATT_GROUPS)):
        for part in range(3):
            lo = part * ATT_WIDTH + g * gw
            slab_order = pair_order if part < 2 else list(range(LANES))
            att_cols += [lo + p * LANES + src for p in range(gw // LANES) for src in slab_order]
    att_cols = jnp.asarray(att_cols, jnp.int32)
    for i in range(depth):
        w_in = w_mix_in[i]
        w_qkv = w_in[:, :c_hgrn].astype(BF16)[:, att_cols]
        w_hgrn = w_in[:, c_hgrn:c_gate].astype(BF16)
        w_gate = w_in[:, c_gate:].astype(BF16)

        h = _ffn_ln(h, ffn1_w_in[i].astype(BF16), ffn1_w_out[i].astype(BF16), vec(ln1_g[i]), vec(ln1_b[i]), alpha)
        for window, dilation in ATT_GROUPS:
            assert window == ATT_BLOCK * dilation
        o_att = _attention(h, pos_rows, w_qkv, batch, seq)
        o_hgrn = _hgrn_scan(h, w_hgrn, lb_logits, vec(hgrn_norm_g[i]), i, batch, seq)
        h = _merge(h, o_att, o_hgrn, w_gate, w_proj_attn[i].astype(BF16), w_proj_hgrn[i].astype(BF16),
                   w_mix_out[i].astype(BF16), vec(ln2_g[i]), vec(ln2_b[i]), alpha)
        h = _ffn_ln(h, ffn2_w_in[i].astype(BF16), ffn2_w_out[i].astype(BF16), vec(ln3_g[i]), vec(ln3_b[i]), alpha,
                    ple=(p[i].reshape(t, -1), ple_w_gate[i].astype(BF16), ple_w_proj[i].astype(BF16)))
    return h.reshape(batch, seq, d_model)
```

```python
import functools
import math

import jax
import jax.numpy as jnp
from jax import lax
from jax.experimental import pallas as pl
from jax.experimental.pallas import tpu as pltpu

F32 = jnp.float32
BF16 = jnp.bfloat16

ATT_HEAD_DIM = 64
ATT_HEADS_PER_GROUP = 8
ATT_GROUPS = ((128, 1), (512, 4), (2048, 16))
ATT_GROUP_WIDTH = ATT_HEADS_PER_GROUP * ATT_HEAD_DIM
ATT_WIDTH = ATT_GROUP_WIDTH * len(ATT_GROUPS)
ATT_BLOCK = 128
ROPE_THETA = 500000.0
ROPE_DIM = ATT_HEAD_DIM // 4
HGRN_HEADS = 8
HGRN_DIM = 128
LN_EPS = 1e-5
RMS_EPS = 1e-6

LANES = 128
SUBLANES = 8
VMEM_LIMIT_BYTES = 62 * 1024 * 1024

TOKEN_TILE = 1024
FF_CHUNK = 256
PROJ_CHUNK = 512
ATT_STEP_BLOCKS = 4
ATT_STEP_ROWS = ATT_STEP_BLOCKS * ATT_BLOCK
ATT_BLOCKS_PER_TRIP = 2
ATT_TILE = ATT_BLOCK * max(d for _, d in ATT_GROUPS)
HGRN_CHUNK = 64
HGRN_STEP_TOKENS = 512


def _params(*sem):
    return pltpu.CompilerParams(dimension_semantics=sem, vmem_limit_bytes=VMEM_LIMIT_BYTES)


def _resident(shape):
    nd = len(shape)
    return pl.BlockSpec(shape, lambda *_: (0,) * nd)


def _layer_norm(y, g, b):
    mu = jnp.mean(y, axis=-1, keepdims=True)
    yc = y - mu
    var = jnp.mean(yc * yc, axis=-1, keepdims=True)
    return yc * lax.rsqrt(var + LN_EPS) * g + b


def _silu(a):
    return a * jax.nn.sigmoid(a)


def _slab_spec(rows, d, index_map, **kw):
    return pl.BlockSpec((d // LANES, rows, LANES), lambda *i: (0, index_map(*i), 0), **kw)


def _load_slabs(ref, rows=slice(None)):
    return jnp.concatenate([ref[c, rows, :] for c in range(ref.shape[0])], axis=1)


def _store_slabs(ref, val, rows=slice(None)):
    for c in range(ref.shape[0]):
        ref[c, rows, :] = val[:, c * LANES:(c + 1) * LANES]


def _dot(a, b):
    return jnp.dot(a, b, preferred_element_type=F32)


def _dot_nt(a, b):
    return lax.dot_general(a, b, (((1,), (1,)), ((), ())), preferred_element_type=F32)


def _dot_tn(a, b):
    return lax.dot_general(a, b, (((0,), (0,)), ((), ())), preferred_element_type=F32)


def _ffn_core(x, win_ref, wout_ref, d_ff):
    xb = x.astype(BF16)
    acc = jnp.zeros(x.shape, F32)
    for c in range(d_ff // FF_CHUNK):
        lo = c * FF_CHUNK
        a = _dot(xb, win_ref[:, lo:lo + FF_CHUNK])
        u = _dot(xb, win_ref[:, d_ff + lo:d_ff + lo + FF_CHUNK])
        acc = acc + _dot((_silu(a) * u).astype(BF16), wout_ref[lo:lo + FF_CHUNK, :])
    return acc


def _ffn_ln_kernel(x_ref, win_ref, wout_ref, g_ref, b_ref, o_ref, *, d_ff, alpha):
    x = x_ref[...]
    y = alpha * x + 0.5 * _ffn_core(x, win_ref, wout_ref, d_ff)
    _store_slabs(o_ref, _layer_norm(y, g_ref[...], b_ref[...]))


def _ffn_ln_ple_kernel(x_ref, p_ref, win_ref, wout_ref, g_ref, b_ref, wpg_ref, wpp_ref, o_ref, *, d_ff, alpha):
    x = x_ref[...]
    y = alpha * x + 0.5 * _ffn_core(x, win_ref, wout_ref, d_ff)
    h = _layer_norm(y, g_ref[...], b_ref[...])
    gate = jax.nn.sigmoid(_dot(h.astype(BF16), wpg_ref[...]))
    emb = _dot(p_ref[...].astype(BF16), wpp_ref[...])
    o_ref[...] = h + gate * emb


def _ffn_ln(x2d, w_in, w_out, g, b, alpha, ple=None):
    t, d = x2d.shape
    d_ff = w_out.shape[0]
    tm = min(TOKEN_TILE, t)
    row = lambda i: (i, 0)
    in_specs = [pl.BlockSpec((tm, d), row)]
    args = [x2d]
    if ple is not None:
        p2d, w_pg, w_pp = ple
        in_specs.append(pl.BlockSpec((tm, p2d.shape[1]), row))
        args.append(p2d)
    in_specs += [_resident(w_in.shape), _resident(w_out.shape), _resident(g.shape), _resident(b.shape)]
    args += [w_in, w_out, g, b]
    if ple is None:
        body = functools.partial(_ffn_ln_kernel, d_ff=d_ff, alpha=alpha)
    else:
        in_specs += [_resident(w_pg.shape), _resident(w_pp.shape)]
        args += [w_pg, w_pp]
        body = functools.partial(_ffn_ln_ple_kernel, d_ff=d_ff, alpha=alpha)
    return pl.pallas_call(
        body,
        grid=(t // tm,),
        in_specs=in_specs,
        out_specs=_slab_spec(tm, d, lambda i: i) if ple is None else pl.BlockSpec((tm, d), row),
        out_shape=jax.ShapeDtypeStruct((d // LANES, t, LANES) if ple is None else (t, d), F32),
        compiler_params=_params("parallel"),
        name="ffn_ln" if ple is None else "ffn_ln_ple",
    )(*args)


def _pair_lane_order():
    hd, half = ATT_HEAD_DIM, ROPE_DIM // 2
    a, b = list(range(hd)), list(range(hd, 2 * hd))
    order = (a[:half] + b[:half] + a[ROPE_DIM:] + a[half:ROPE_DIM] + b[half:ROPE_DIM] + b[ROPE_DIM:])
    return order, [src < hd for src in order]


def _rope_tables(pos_row_f32):
    half = ROPE_DIM // 2
    assert half == SUBLANES
    freq = lax.broadcasted_iota(jnp.int32, (half, 1), 0).astype(F32)
    ang = jnp.exp(-math.log(ROPE_THETA) * freq * (2.0 / ROPE_DIM)) * pos_row_f32
    k_row = lax.broadcasted_iota(jnp.int32, (4 * half, LANES), 0)
    k_lane = lax.broadcasted_iota(jnp.int32, (4 * half, LANES), 1)
    take = jnp.logical_and((k_row & (half - 1)) == (k_lane & (half - 1)), k_row < 3 * half)
    spread_matrix = jnp.where(take, 1.0, 0.0).astype(BF16)

    def spread(c8):
        hi = c8.astype(BF16).astype(F32)
        mid = (c8 - hi).astype(BF16).astype(F32)
        lo = c8 - hi - mid
        terms = jnp.concatenate([hi, mid, lo, jnp.zeros_like(hi)], axis=0).astype(BF16)
        return _dot_tn(terms, spread_matrix)

    lane = lax.broadcasted_iota(jnp.int32, (1, LANES), 1)
    rotary = (lane & (ATT_HEAD_DIM - 1)) < ROPE_DIM
    sign = jnp.where(lane < ATT_HEAD_DIM, -1.0, 1.0)
    return jnp.where(rotary, spread(jnp.cos(ang)), 1.0), jnp.where(rotary, spread(jnp.sin(ang)) * sign, 0.0)


def _merge_lse(o_a, lse_a, o_b, lse_b, want_lse):
    m = jnp.maximum(lse_a, lse_b)
    w_a = jnp.exp(lse_a - m)
    w_b = jnp.exp(lse_b - m)
    den = w_a + w_b
    o = (w_a * o_a + w_b * o_b) / den
    return o, (m + jnp.log(den) if want_lse else None)


def _attn_fused_kernel(x_ref, pos_ref, w_ref, o_ref, cos_s, sin_s, xg, cos_g, sin_g, q_buf, kv_ext,
                       carry_near, carry_mid, carry_far, step_o, step_lse, acc_o, acc_lse):
    n = pl.program_id(1)
    blk, gw = ATT_BLOCK, ATT_GROUP_WIDTH
    (_, d_near), (_, d_mid), (_, d_far) = ATT_GROUPS
    region = ATT_TILE // d_mid
    assert d_near == 1 and d_far == d_mid * d_mid and region == ATT_STEP_ROWS and d_mid == ATT_STEP_BLOCKS

    @pl.when(n == 0)
    def _():
        carry_near[...] = jnp.zeros_like(carry_near)
        carry_mid[...] = jnp.zeros_like(carry_mid)
        carry_far[...] = jnp.zeros_like(carry_far)

    cos_s[...], sin_s[...] = _rope_tables(pos_ref[...].astype(F32))

    row = lax.broadcasted_iota(jnp.int32, (blk, blk), 0)
    col = lax.broadcasted_iota(jnp.int32, (blk, blk), 1)
    neg_inf = jnp.float32(-jnp.inf)
    bias_own = jnp.where(col <= row, 0.0, neg_inf)
    bias_prev = jnp.where(col >= row, 0.0, neg_inf)
    lane = lax.broadcasted_iota(jnp.int32, (1, LANES), 1)
    first = lane < ATT_HEAD_DIM
    lane_j = lane & (ATT_HEAD_DIM - 1)
    half = ROPE_DIM // 2
    first_qk = jnp.logical_or(lane_j < half, jnp.logical_and(lane < ATT_HEAD_DIM, lane_j >= ROPE_DIM))
    ones_a = jnp.broadcast_to(jnp.where(first, 1.0, 0.0).astype(BF16), (blk, LANES))
    ones_b = jnp.broadcast_to(jnp.where(first, 0.0, 1.0).astype(BF16), (blk, LANES))
    rep = gw // LANES

    pending = []

    def flush(parts_left):
        for _ in range(-(-len(pending) // parts_left)):
            pending.pop(0)()

    def project(group, xb, cos, sin):
        for part in range(3):
            lo = (3 * group + part) * gw
            z = _dot(xb, w_ref[:, lo:lo + gw])
            flush(3 - part)
            if part < 2:
                slabs = []
                for c in range(rep):
                    zc = z[:, c * LANES:(c + 1) * LANES]
                    slabs.append(zc * cos + pltpu.roll(zc, ATT_HEAD_DIM, 1) * sin)
                z = jnp.concatenate(slabs, axis=1)
            if part == 0:
                q_buf[...] = (z * ATT_HEAD_DIM ** -0.5).astype(BF16)
            else:
                kv_ext[blk:blk + ATT_STEP_ROWS, (part - 1) * gw:part * gw] = z.astype(BF16)

    def run_blocks(prev_ref, prev_base, prev_valid):
        def scores(a):
            off = pl.multiple_of(a * blk, blk)
            q = q_buf[pl.ds(off, blk), :]
            own = kv_ext[pl.ds(off + blk, blk), 0:gw]
            prev = prev_ref[pl.ds(prev_base + off, blk), 0:gw]
            bias = jnp.concatenate([jnp.where(prev_valid(a), bias_prev, neg_inf), bias_own], axis=1)
            zero = jnp.zeros((blk, LANES), BF16)
            e_all, m_all = [], []
            for p in range(rep):
                sl = slice(p * LANES, (p + 1) * LANES)
                q2 = q[:, sl]
                q_cat = jnp.concatenate([jnp.where(first_qk, q2, zero), jnp.where(first_qk, zero, q2)], axis=0)
                k_cat = jnp.concatenate([prev[:, sl], own[:, sl]], axis=0)
                s = _dot_nt(q_cat, k_cat)
                e_parts, m_parts = [], []
                for hd in range(2):
                    s_h = s[hd * blk:(hd + 1) * blk, :] + bias
                    m = jnp.max(s_h, axis=1, keepdims=True)
                    e_parts.append(jnp.exp(s_h - m).astype(BF16))
                    m_parts.append(m)
                e_all.append(jnp.concatenate(e_parts, axis=1))
                m_all.append(jnp.where(first, m_parts[0], m_parts[1]))
            return e_all, m_all

        def values(a, e_all, m_all):
            off = pl.multiple_of(a * blk, blk)
            own = kv_ext[pl.ds(off + blk, blk), gw:2 * gw]
            prev = prev_ref[pl.ds(prev_base + off, blk), gw:2 * gw]
            zero = jnp.zeros((blk, LANES), BF16)
            for p in range(rep):
                sl = slice(p * LANES, (p + 1) * LANES)
                vp2, vo2 = prev[:, sl], own[:, sl]
                v_cat = jnp.concatenate([
                    jnp.concatenate([jnp.where(first, vp2, zero), ones_a], axis=1),
                    jnp.concatenate([jnp.where(first, vo2, zero), ones_a], axis=1),
                    jnp.concatenate([jnp.where(first, zero, vp2), ones_b], axis=1),
                    jnp.concatenate([jnp.where(first, zero, vo2), ones_b], axis=1)], axis=0)
                r = _dot(e_all[p], v_cat)
                den = r[:, LANES:]
                step_o[p, pl.ds(off, blk), :] = r[:, :LANES] / den
                step_lse[p, pl.ds(off, blk), :] = m_all[p] + jnp.log(den)

        def body(i, carry):
            blocks = [ATT_BLOCKS_PER_TRIP * i + u for u in range(ATT_BLOCKS_PER_TRIP)]
            staged = [scores(a) for a in blocks]
            for a, (e_all, m_all) in zip(blocks, staged):
                values(a, e_all, m_all)
            return carry

        lax.fori_loop(0, ATT_STEP_BLOCKS // ATT_BLOCKS_PER_TRIP, body, 0)

    def rows_of(ref3, rows):
        return jnp.concatenate([ref3[c, rows, :] for c in range(ref3.shape[0])], axis=1)

    seen_tile = n > 0
    chained = lambda a: jnp.logical_or(a > 0, seen_tile)

    def keep_near(j, c):
        for rho in range(d_mid):
            src = pl.ds(rho, blk, stride=d_mid)
            dst = slice(rho * region + j * blk, rho * region + (j + 1) * blk)
            acc_o[c, dst, :] = step_o[c, src, :]
            acc_lse[c, dst, :] = step_lse[c, src, :]

    def merge_mid(rho, c):
        reg = slice(rho * region, (rho + 1) * region)
        o, lse = _merge_lse(acc_o[c, reg, :], acc_lse[c, reg, :], step_o[c], step_lse[c], True)
        acc_o[c, reg, :] = o
        acc_lse[c, reg, :] = lse

    def merge_far(rho, c):
        for a in range(ATT_STEP_BLOCKS):
            dst = pl.ds(rho * region + a, blk, stride=d_mid)
            blk_rows = slice(a * blk, (a + 1) * blk)
            o, _ = _merge_lse(acc_o[c, dst, :], acc_lse[c, dst, :], step_o[c, blk_rows, :], step_lse[c, blk_rows, :], False)
            acc_o[c, dst, :] = o
        o_ref[c, pl.ds(rho, region, stride=d_mid), :] = acc_o[c, rho * region:(rho + 1) * region, :]

    def later(fn, *args):
        pending.extend(functools.partial(fn, *args, c) for c in range(rep))

    kv_ext[0:blk, :] = carry_near[...]
    for j in range(ATT_TILE // ATT_STEP_ROWS):
        rows = slice(j * ATT_STEP_ROWS, (j + 1) * ATT_STEP_ROWS)
        project(0, rows_of(x_ref, rows).astype(BF16), cos_s[rows, :], sin_s[rows, :])
        run_blocks(kv_ext, 0, (lambda a: True) if j > 0 else chained)
        kv_ext[0:blk, :] = kv_ext[ATT_STEP_ROWS:ATT_STEP_ROWS + blk, :]
        later(keep_near, j)
    carry_near[...] = kv_ext[0:blk, :]

    for rho in range(d_mid):
        src = pl.ds(rho, region, stride=d_mid)
        for c in range(xg.shape[0]):
            xg[c] = x_ref[c, src, :]
        cos_g[...] = cos_s[src, :]
        sin_g[...] = sin_s[src, :]

        kv_ext[0:blk, :] = carry_mid[rho * blk:(rho + 1) * blk, :]
        project(1, rows_of(xg, slice(None)).astype(BF16), cos_g[...], sin_g[...])
        run_blocks(kv_ext, 0, chained)
        carry_mid[rho * blk:(rho + 1) * blk, :] = kv_ext[ATT_STEP_ROWS:ATT_STEP_ROWS + blk, :]
        later(merge_mid, rho)

        far = [pl.ds(a, blk, stride=d_mid) for a in range(ATT_STEP_BLOCKS)]
        project(2, jnp.concatenate([rows_of(xg, f) for f in far], axis=0).astype(BF16),
                jnp.concatenate([cos_g[f, :] for f in far], axis=0), jnp.concatenate([sin_g[f, :] for f in far], axis=0))
        run_blocks(carry_far, rho * ATT_STEP_ROWS, lambda a: seen_tile)
        carry_far[rho * ATT_STEP_ROWS:(rho + 1) * ATT_STEP_ROWS, :] = kv_ext[blk:blk + ATT_STEP_ROWS, :]
        later(merge_far, rho)
    flush(1)


def _attention(h_slabs, pos_rows, w_qkv, batch, seq):
    t, d = h_slabs.shape[1], w_qkv.shape[0]
    gw = ATT_GROUP_WIDTH
    assert seq % ATT_TILE == 0
    tiles = seq // ATT_TILE
    tile = lambda b, n: b * tiles + n
    kv_rows = ATT_BLOCK + ATT_STEP_ROWS
    d_mid, d_far = ATT_GROUPS[1][1], ATT_GROUPS[2][1]
    return pl.pallas_call(
        _attn_fused_kernel,
        grid=(batch, tiles),
        in_specs=[
            _slab_spec(ATT_TILE, d, tile),
            pl.BlockSpec((None, 1, ATT_TILE), lambda b, n: (tile(b, n), 0, 0)),
            _resident(w_qkv.shape),
        ],
        out_specs=_slab_spec(ATT_TILE, gw, tile),
        out_shape=jax.ShapeDtypeStruct((gw // LANES, t, LANES), F32),
        scratch_shapes=[
            pltpu.VMEM((ATT_TILE, LANES), F32),
            pltpu.VMEM((ATT_TILE, LANES), F32),
            pltpu.VMEM((d // LANES, ATT_STEP_ROWS, LANES), F32),
            pltpu.VMEM((ATT_STEP_ROWS, LANES), F32),
            pltpu.VMEM((ATT_STEP_ROWS, LANES), F32),
            pltpu.VMEM((ATT_STEP_ROWS, gw), BF16),
            pltpu.VMEM((kv_rows, 2 * gw), BF16),
            pltpu.VMEM((ATT_BLOCK, 2 * gw), BF16),
            pltpu.VMEM((d_mid * ATT_BLOCK, 2 * gw), BF16),
            pltpu.VMEM((d_far * ATT_BLOCK, 2 * gw), BF16),
            pltpu.VMEM((gw // LANES, ATT_STEP_ROWS, LANES), F32),
            pltpu.VMEM((gw // LANES, ATT_STEP_ROWS, LANES), F32),
            pltpu.VMEM((gw // LANES, ATT_TILE, LANES), F32),
            pltpu.VMEM((gw // LANES, ATT_TILE, LANES), F32),
        ],
        compiler_params=_params("parallel", "arbitrary"),
        name="attn_fused",
    )(h_slabs, pos_rows, w_qkv)


def _split2(x):
    hi = x.astype(BF16)
    lo = (x - hi.astype(F32)).astype(BF16)
    return hi, lo


def _rows_bcast(x, row_in_block):
    parts = [jnp.broadcast_to(x[i + row_in_block:i + row_in_block + 1, :], (SUBLANES, x.shape[1]))
             for i in range(0, x.shape[0], SUBLANES)]
    return jnp.concatenate(parts, axis=0)


def _hgrn_kernel(lbl_ref, ng_ref, x_ref, w_ref, o_ref, z_s, *st_refs, layer, n_chunks):
    c = HGRN_CHUNK
    width = HGRN_HEADS * HGRN_DIM
    blk = SUBLANES
    nblk = c // blk
    kinds = {"q": 0, "f": 1, "i": 2, "g": 3}

    @pl.when(pl.program_id(1) == 0)
    def _():
        for st_ref in st_refs:
            st_ref[...] = jnp.zeros_like(st_ref)

    lgt = lbl_ref[...]
    ex = jnp.exp(lgt - jnp.max(lgt, axis=0, keepdims=True))
    sm = ex / jnp.sum(ex, axis=0, keepdims=True)
    lb = jnp.sum(sm[0:layer + 1, :], axis=0, keepdims=True)

    r_i = lax.broadcasted_iota(jnp.int32, (c, c), 0)
    c_i = lax.broadcasted_iota(jnp.int32, (c, c), 1)
    tri = (c_i <= r_i).astype(BF16)
    rowid = lax.broadcasted_iota(jnp.int32, (c, 1), 0)
    second = {hs: (rowid & hs) != 0 for hs in (1, 2, 4)}
    pair = {hs: jnp.logical_and((r_i // (2 * hs)) == (c_i // (2 * hs)),
                                jnp.logical_and((r_i & hs) != 0, (c_i & hs) == 0)) for hs in (1, 2, 4)}
    diag = r_i == c_i
    ng = ng_ref[...]
    zeros_blk = jnp.zeros((blk, HGRN_DIM), F32)
    heads = [slice(h * HGRN_DIM, (h + 1) * HGRN_DIM) for h in range(HGRN_HEADS)]

    xb = _load_slabs(x_ref).astype(BF16)

    def project(kind):
        lo = kinds[kind] * width
        for j in range(width // PROJ_CHUNK):
            sl = slice(lo + j * PROJ_CHUNK, lo + (j + 1) * PROJ_CHUNK)
            z_s[:, sl] = _dot(xb, w_ref[:, sl])

    def pre(kind, ci):
        lo = kinds[kind] * width
        return z_s[ci * c:(ci + 1) * c, lo:lo + width]

    def decay_part(ci):
        f = lb + (1.0 - lb) * jax.nn.sigmoid(pre("f", ci))
        hi, lo = _split2(jnp.log(f))
        b = _dot(tri, hi) + _dot(tri, lo)
        return f, b

    def chunk(ci, f, b):
        rows = slice(ci * c, (ci + 1) * c)
        k = 1.0 - f
        q = _silu(pre("q", ci))
        v = pre("i", ci)
        b_last = b[c - 1:c, :]
        qs = (q * jnp.exp(b)).astype(BF16)
        ks = (k * jnp.exp(b_last - b)).astype(BF16)
        decay = jnp.exp(b_last)
        qb, kb, vb = q.astype(BF16), k.astype(BF16), v.astype(BF16)

        q_far, k_far = [], []
        for j in range(nblk - 1):
            lo_r = (j + 1) * blk
            b_end = b[lo_r - 1:lo_r, :]
            q_far.append(q[lo_r:, :] * jnp.exp(b[lo_r:, :] - b_end))
            k_far.append(k[lo_r - blk:lo_r, :] * jnp.exp(b_end - b[lo_r - blk:lo_r, :]))

        lvl_q, lvl_k = {}, {}
        for hs, bref in ((4, _rows_bcast(b, 3)),
                         (2, jnp.where(second[4], _rows_bcast(b, 5), _rows_bcast(b, 1)))):
            x = jnp.exp(-jnp.abs(b - bref))
            lvl_q[hs] = jnp.where(second[hs], q * x, 0.0).astype(BF16)
            lvl_k[hs] = jnp.where(second[hs], 0.0, k * x).astype(BF16)
        lvl_q[1] = jnp.where(second[1], q * f, 0.0).astype(BF16)
        lvl_k[1] = jnp.where(second[1], 0.0, k).astype(BF16)

        o_inter, att_all = [], []
        for h, sl in enumerate(heads):
            st = st_refs[h][...]
            o_inter.append(_dot_nt(qs[:, sl], st.astype(BF16)))
            q_cat = jnp.concatenate(
                [jnp.concatenate([jnp.zeros(((j + 1) * blk, HGRN_DIM), F32), q_far[j][:, sl]], axis=0)
                 for j in range(nblk - 1)], axis=1).astype(BF16)
            k_cat = jnp.concatenate(
                [jnp.concatenate([zeros_blk] * j + [k_far[j][:, sl]] + [zeros_blk] * (nblk - 1 - j), axis=0)
                 for j in range(nblk - 1)], axis=1).astype(BF16)
            att = _dot_nt(q_cat, k_cat)
            for hs in (4, 2, 1):
                att = att + jnp.where(pair[hs], _dot_nt(lvl_q[hs][:, sl], lvl_k[hs][:, sl]), 0.0)
            att = att + jnp.where(diag, _dot_nt(qb[:, sl], kb[:, sl]), 0.0)
            att_all.append(att.astype(BF16))
            st_refs[h][...] = st * decay[:, sl] + _dot_tn(vb[:, sl], ks[:, sl])
        g_act = _silu(pre("g", ci))
        for h, sl in enumerate(heads):
            o = o_inter[h] + _dot(att_all[h], vb[:, sl])
            o = o * lax.rsqrt(jnp.mean(o * o, axis=1, keepdims=True) + RMS_EPS)
            o_ref[rows, sl] = (o * ng[:, sl] * g_act[:, sl]).astype(o_ref.dtype)

    project("f")
    decays = [decay_part(ci) for ci in range(n_chunks)]
    project("q")
    project("i")
    project("g")
    for ci, (f, b) in enumerate(decays):
        chunk(ci, f, b)


def _hgrn_scan(h_slabs, w_hgrn, lb_logits, norm_g, layer, batch, seq):
    width = HGRN_HEADS * HGRN_DIM
    d = w_hgrn.shape[0]
    ts = min(HGRN_STEP_TOKENS, seq)
    steps = seq // ts
    body = functools.partial(_hgrn_kernel, layer=layer, n_chunks=ts // HGRN_CHUNK)
    return pl.pallas_call(
        body,
        grid=(batch, steps),
        in_specs=[
            _resident(lb_logits.shape),
            _resident(norm_g.shape),
            _slab_spec(ts, d, lambda b, s: b * steps + s),
            _resident(w_hgrn.shape),
        ],
        out_specs=pl.BlockSpec((ts, width), lambda b, s: (b * steps + s, 0)),
        out_shape=jax.ShapeDtypeStruct((batch * seq, width), BF16),
        scratch_shapes=[pltpu.VMEM((ts, w_hgrn.shape[1]), F32)] + [pltpu.VMEM((HGRN_DIM, HGRN_DIM), F32)] * HGRN_HEADS,
        compiler_params=_params("parallel", "arbitrary"),
        name="hgrn_scan",
    )(lb_logits, norm_g, h_slabs, w_hgrn)


def _merge_kernel(h_ref, oa_ref, oh_ref, wg_ref, wpa_ref, wpb_ref, wo_ref, g_ref, b_ref, out_ref, *, alpha):
    h = _load_slabs(h_ref)
    d = h.shape[1]
    hb = h.astype(BF16)
    y_att = _dot(_load_slabs(oa_ref).astype(BF16), wpa_ref[...])
    y_hgrn = _dot(oh_ref[...], wpb_ref[...])
    merged = (jax.nn.sigmoid(_dot(hb, wg_ref[:, 0:d])) * y_att + jax.nn.sigmoid(_dot(hb, wg_ref[:, d:2 * d])) * y_hgrn)
    mix = _dot(merged.astype(BF16), wo_ref[...])
    out_ref[...] = _layer_norm(alpha * h + mix, g_ref[...], b_ref[...])


def _merge(h_slabs, o_att, o_hgrn, w_gate, w_pa, w_pb, w_o, g, b, alpha):
    t, d = o_hgrn.shape
    tm = min(TOKEN_TILE, t)
    row = lambda i: (i, 0)
    in_specs = [_slab_spec(tm, d, lambda i: i),
                _slab_spec(tm, o_att.shape[0] * LANES, lambda i: i),
                pl.BlockSpec((tm, d), row)]
    in_specs += [_resident(a.shape) for a in (w_gate, w_pa, w_pb, w_o, g, b)]
    return pl.pallas_call(
        functools.partial(_merge_kernel, alpha=alpha),
        grid=(t // tm,),
        in_specs=in_specs,
        out_specs=pl.BlockSpec((tm, d), row),
        out_shape=jax.ShapeDtypeStruct((t, d), F32),
        compiler_params=_params("parallel"),
        name="merge_ln",
    )(h_slabs, o_att, o_hgrn, w_gate, w_pa, w_pb, w_o, g, b)


def kernel(x, p, positions, ffn1_w_in, ffn1_w_out, ln1_g, ln1_b, w_mix_in, hgrn_lb_logits, hgrn_norm_g, w_proj_attn, w_proj_hgrn, w_mix_out, ln2_g, ln2_b, ffn2_w_in, ffn2_w_out, ln3_g, ln3_b, ple_w_proj, ple_w_gate):
    batch, seq, d_model = x.shape
    depth = ffn1_w_in.shape[0]
    alpha = (2 * depth) ** 0.25
    width = HGRN_HEADS * HGRN_DIM
    gw = ATT_GROUP_WIDTH
    t = batch * seq
    h = x.reshape(t, d_model)
    pos_rows = positions.reshape(t // ATT_TILE, 1, ATT_TILE)
    lb_logits = hgrn_lb_logits.astype(F32)
    vec = lambda a: a.reshape(1, -1).astype(F32)
    c_hgrn = 3 * ATT_WIDTH
    c_gate = c_hgrn + 4 * width
    pair_order, _ = _pair_lane_order()
    att_cols = []
    for g in range(len(ATT_GROUPS)):
        for part in range(3):
            lo = part * ATT_WIDTH + g * gw
            slab_order = pair_order if part < 2 else list(range(LANES))
            att_cols += [lo + p * LANES + src for p in range(gw // LANES) for src in slab_order]
    att_cols = jnp.asarray(att_cols, jnp.int32)
    for i in range(depth):
        w_in = w_mix_in[i]
        w_qkv = w_in[:, :c_hgrn].astype(BF16)[:, att_cols]
        w_hgrn = w_in[:, c_hgrn:c_gate].astype(BF16)
        w_gate = w_in[:, c_gate:].astype(BF16)

        h = _ffn_ln(h, ffn1_w_in[i].astype(BF16), ffn1_w_out[i].astype(BF16), vec(ln1_g[i]), vec(ln1_b[i]), alpha)
        for window, dilation in ATT_GROUPS:
            assert window == ATT_BLOCK * dilation
        o_att = _attention(h, pos_rows, w_qkv, batch, seq)
        o_hgrn = _hgrn_scan(h, w_hgrn, lb_logits, vec(hgrn_norm_g[i]), i, batch, seq)
        h = _merge(h, o_att, o_hgrn, w_gate, w_proj_attn[i].astype(BF16), w_proj_hgrn[i].astype(BF16),
                   w_mix_out[i].astype(BF16), vec(ln2_g[i]), vec(ln2_b[i]), alpha)
        h = _ffn_ln(h, ffn2_w_in[i].astype(BF16), ffn2_w_out[i].astype(BF16), vec(ln3_g[i]), vec(ln3_b[i]), alpha,
                    ple=(p[i].reshape(t, -1), ple_w_gate[i].astype(BF16), ple_w_proj[i].astype(BF16)))
    return h.reshape(batch, seq, d_model)
```

```python
import functools
import math

import jax
import jax.numpy as jnp
from jax import lax
from jax.experimental import pallas as pl
from jax.experimental.pallas import tpu as pltpu

F32 = jnp.float32
BF16 = jnp.bfloat16

ATT_HEAD_DIM = 64
ATT_HEADS_PER_GROUP = 8
ATT_GROUPS = ((128, 1), (512, 4), (2048, 16))
ATT_GROUP_WIDTH = ATT_HEADS_PER_GROUP * ATT_HEAD_DIM
ATT_WIDTH = ATT_GROUP_WIDTH * len(ATT_GROUPS)
ATT_BLOCK = 128
ROPE_THETA = 500000.0
ROPE_DIM = ATT_HEAD_DIM // 4
HGRN_HEADS = 8
HGRN_DIM = 128
LN_EPS = 1e-5
RMS_EPS = 1e-6

LANES = 128
SUBLANES = 8
VMEM_LIMIT_BYTES = 62 * 1024 * 1024

TOKEN_TILE = 1024
FF_CHUNK = 256
PROJ_CHUNK = 512
ATT_STEP_BLOCKS = 4
ATT_STEP_ROWS = ATT_STEP_BLOCKS * ATT_BLOCK
ATT_BLOCKS_PER_TRIP = 2
ATT_TILE = ATT_BLOCK * max(d for _, d in ATT_GROUPS)
HGRN_CHUNK = 64
HGRN_STEP_TOKENS = 512


def _params(*sem):
    return pltpu.CompilerParams(dimension_semantics=sem, vmem_limit_bytes=VMEM_LIMIT_BYTES)


def _resident(shape):
    nd = len(shape)
    return pl.BlockSpec(shape, lambda *_: (0,) * nd)


def _layer_norm(y, g, b):
    mu = jnp.mean(y, axis=-1, keepdims=True)
    yc = y - mu
    var = jnp.mean(yc * yc, axis=-1, keepdims=True)
    return yc * lax.rsqrt(var + LN_EPS) * g + b


def _silu(a):
    return a * jax.nn.sigmoid(a)


def _slab_spec(rows, d, index_map, **kw):
    return pl.BlockSpec((d // LANES, rows, LANES), lambda *i: (0, index_map(*i), 0), **kw)


def _load_slabs(ref, rows=slice(None)):
    return jnp.concatenate([ref[c, rows, :] for c in range(ref.shape[0])], axis=1)


def _store_slabs(ref, val, rows=slice(None)):
    for c in range(ref.shape[0]):
        ref[c, rows, :] = val[:, c * LANES:(c + 1) * LANES]


def _dot(a, b):
    return jnp.dot(a, b, preferred_element_type=F32)


def _dot_nt(a, b):
    return lax.dot_general(a, b, (((1,), (1,)), ((), ())), preferred_element_type=F32)


def _dot_tn(a, b):
    return lax.dot_general(a, b, (((0,), (0,)), ((), ())), preferred_element_type=F32)


def _ffn_core(x, win_ref, wout_ref, d_ff):
    xb = x.astype(BF16)
    acc = jnp.zeros(x.shape, F32)
    for lo in range(0, d_ff, FF_CHUNK):
        hi = min(lo + FF_CHUNK, d_ff)
        a = _dot(xb, win_ref[:, lo:hi].astype(BF16))
        u = _dot(xb, win_ref[:, d_ff + lo:d_ff + hi].astype(BF16))
        acc = acc + _dot((_silu(a) * u).astype(BF16), wout_ref[lo:hi, :])
    return acc


def _ffn_ln_kernel(x_ref, win_ref, wout_ref, g_ref, b_ref, o_ref, *, d_ff, alpha):
    x = x_ref[...]
    y = alpha * x + 0.5 * _ffn_core(x, win_ref, wout_ref, d_ff)
    _store_slabs(o_ref, _layer_norm(y, g_ref[...], b_ref[...]))


def _ffn_ln_ple_kernel(x_ref, p_ref, win_ref, wout_ref, g_ref, b_ref, wpg_ref, wpp_ref, o_ref, *, d_ff, alpha):
    x = x_ref[...]
    y = alpha * x + 0.5 * _ffn_core(x, win_ref, wout_ref, d_ff)
    h = _layer_norm(y, g_ref[...], b_ref[...])
    gate = jax.nn.sigmoid(_dot(h.astype(BF16), wpg_ref[...]))
    emb = _dot(p_ref[...].astype(BF16), wpp_ref[...])
    o_ref[...] = h + gate * emb


def _ffn_ln(x2d, w_in, w_out, g, b, alpha, ple=None):
    t, d = x2d.shape
    d_ff = w_out.shape[0]
    tm = min(TOKEN_TILE, t)
    row = lambda i: (i, 0)
    in_specs = [pl.BlockSpec((tm, d), row)]
    args = [x2d]
    if ple is not None:
        p2d, w_pg, w_pp = ple
        in_specs.append(pl.BlockSpec((tm, p2d.shape[1]), row))
        args.append(p2d)
    in_specs += [_resident(w_in.shape), _resident(w_out.shape), _resident(g.shape), _resident(b.shape)]
    args += [w_in, w_out, g, b]
    if ple is None:
        body = functools.partial(_ffn_ln_kernel, d_ff=d_ff, alpha=alpha)
    else:
        in_specs += [_resident(w_pg.shape), _resident(w_pp.shape)]
        args += [w_pg, w_pp]
        body = functools.partial(_ffn_ln_ple_kernel, d_ff=d_ff, alpha=alpha)
    return pl.pallas_call(
        body,
        grid=(t // tm,),
        in_specs=in_specs,
        out_specs=_slab_spec(tm, d, lambda i: i) if ple is None else pl.BlockSpec((tm, d), row),
        out_shape=jax.ShapeDtypeStruct((d // LANES, t, LANES) if ple is None else (t, d), F32),
        compiler_params=_params("parallel"),
        name="ffn_ln" if ple is None else "ffn_ln_ple",
    )(*args)


def _pair_lane_order():
    hd, half = ATT_HEAD_DIM, ROPE_DIM // 2
    a, b = list(range(hd)), list(range(hd, 2 * hd))
    order = (a[:half] + b[:half] + a[ROPE_DIM:] + a[half:ROPE_DIM] + b[half:ROPE_DIM] + b[ROPE_DIM:])
    return order, [src < hd for src in order]


def _rope_tables(pos_row_f32):
    half = ROPE_DIM // 2
    assert half == SUBLANES
    freq = lax.broadcasted_iota(jnp.int32, (half, 1), 0).astype(F32)
    ang = jnp.exp(-math.log(ROPE_THETA) * freq * (2.0 / ROPE_DIM)) * pos_row_f32
    k_row = lax.broadcasted_iota(jnp.int32, (4 * half, LANES), 0)
    k_lane = lax.broadcasted_iota(jnp.int32, (4 * half, LANES), 1)
    take = jnp.logical_and((k_row & (half - 1)) == (k_lane & (half - 1)), k_row < 3 * half)
    spread_matrix = jnp.where(take, 1.0, 0.0).astype(BF16)

    def spread(c8):
        hi = c8.astype(BF16).astype(F32)
        mid = (c8 - hi).astype(BF16).astype(F32)
        lo = c8 - hi - mid
        terms = jnp.concatenate([hi, mid, lo, jnp.zeros_like(hi)], axis=0).astype(BF16)
        return _dot_tn(terms, spread_matrix)

    lane = lax.broadcasted_iota(jnp.int32, (1, LANES), 1)
    rotary = (lane & (ATT_HEAD_DIM - 1)) < ROPE_DIM
    sign = jnp.where(lane < ATT_HEAD_DIM, -1.0, 1.0)
    return jnp.where(rotary, spread(jnp.cos(ang)), 1.0), jnp.where(rotary, spread(jnp.sin(ang)) * sign, 0.0)


def _merge_lse(o_a, lse_a, o_b, lse_b, want_lse):
    m = jnp.maximum(lse_a, lse_b)
    w_a = jnp.exp(lse_a - m)
    w_b = jnp.exp(lse_b - m)
    den = w_a + w_b
    o = (w_a * o_a + w_b * o_b) / den
    return o, (m + jnp.log(den) if want_lse else None)


def _attn_fused_kernel(x_ref, pos_ref, w_ref, o_ref, cos_s, sin_s, xg, cos_g, sin_g, q_buf, kv_ext,
                       carry_near, carry_mid, carry_far, step_o, step_lse, acc_o, acc_lse):
    n = pl.program_id(1)
    blk, gw = ATT_BLOCK, ATT_GROUP_WIDTH
    (_, d_near), (_, d_mid), (_, d_far) = ATT_GROUPS
    region = ATT_TILE // d_mid
    assert d_near == 1 and d_far == d_mid * d_mid and region == ATT_STEP_ROWS and d_mid == ATT_STEP_BLOCKS

    @pl.when(n == 0)
    def _():
        carry_near[...] = jnp.zeros_like(carry_near)
        carry_mid[...] = jnp.zeros_like(carry_mid)
        carry_far[...] = jnp.zeros_like(carry_far)

    cos_s[...], sin_s[...] = _rope_tables(pos_ref[...].astype(F32))

    row = lax.broadcasted_iota(jnp.int32, (blk, blk), 0)
    col = lax.broadcasted_iota(jnp.int32, (blk, blk), 1)
    neg_inf = jnp.float32(-jnp.inf)
    bias_own = jnp.where(col <= row, 0.0, neg_inf)
    bias_prev = jnp.where(col >= row, 0.0, neg_inf)
    lane = lax.broadcasted_iota(jnp.int32, (1, LANES), 1)
    first = lane < ATT_HEAD_DIM
    lane_j = lane & (ATT_HEAD_DIM - 1)
    half = ROPE_DIM // 2
    first_qk = jnp.logical_or(lane_j < half, jnp.logical_and(lane < ATT_HEAD_DIM, lane_j >= ROPE_DIM))
    ones_a = jnp.broadcast_to(jnp.where(first, 1.0, 0.0).astype(BF16), (blk, LANES))
    ones_b = jnp.broadcast_to(jnp.where(first, 0.0, 1.0).astype(BF16), (blk, LANES))
    rep = gw // LANES

    pending = []

    def flush(parts_left):
        for _ in range(-(-len(pending) // parts_left)):
            pending.pop(0)()

    def project(group, xb, cos, sin):
        for part in range(3):
            lo = (3 * group + part) * gw
            z = _dot(xb, w_ref[:, lo:lo + gw])
            flush(3 - part)
            if part < 2:
                slabs = []
                for c in range(rep):
                    zc = z[:, c * LANES:(c + 1) * LANES]
                    slabs.append(zc * cos + pltpu.roll(zc, ATT_HEAD_DIM, 1) * sin)
                z = jnp.concatenate(slabs, axis=1)
            if part == 0:
                q_buf[...] = (z * ATT_HEAD_DIM ** -0.5).astype(BF16)
            else:
                kv_ext[blk:blk + ATT_STEP_ROWS, (part - 1) * gw:part * gw] = z.astype(BF16)

    def run_blocks(prev_ref, prev_base, prev_valid):
        def scores(a):
            off = pl.multiple_of(a * blk, blk)
            q = q_buf[pl.ds(off, blk), :]
            own = kv_ext[pl.ds(off + blk, blk), 0:gw]
            prev = prev_ref[pl.ds(prev_base + off, blk), 0:gw]
            bias = jnp.concatenate([jnp.where(prev_valid(a), bias_prev, neg_inf), bias_own], axis=1)
            zero = jnp.zeros((blk, LANES), BF16)
            e_all, m_all = [], []
            for p in range(rep):
                sl = slice(p * LANES, (p + 1) * LANES)
                q2 = q[:, sl]
                q_cat = jnp.concatenate([jnp.where(first_qk, q2, zero), jnp.where(first_qk, zero, q2)], axis=0)
                k_cat = jnp.concatenate([prev[:, sl], own[:, sl]], axis=0)
                s = _dot_nt(q_cat, k_cat)
                e_parts, m_parts = [], []
                for hd in range(2):
                    s_h = s[hd * blk:(hd + 1) * blk, :] + bias
                    m = jnp.max(s_h, axis=1, keepdims=True)
                    e_parts.append(jnp.exp(s_h - m).astype(BF16))
                    m_parts.append(m)
                e_all.append(jnp.concatenate(e_parts, axis=1))
                m_all.append(jnp.where(first, m_parts[0], m_parts[1]))
            return e_all, m_all

        def values(a, e_all, m_all):
            off = pl.multiple_of(a * blk, blk)
            own = kv_ext[pl.ds(off + blk, blk), gw:2 * gw]
            prev = prev_ref[pl.ds(prev_base + off, blk), gw:2 * gw]
            zero = jnp.zeros((blk, LANES), BF16)
            for p in range(rep):
                sl = slice(p * LANES, (p + 1) * LANES)
                vp2, vo2 = prev[:, sl], own[:, sl]
                v_cat = jnp.concatenate([
                    jnp.concatenate([jnp.where(first, vp2, zero), ones_a], axis=1),
                    jnp.concatenate([jnp.where(first, vo2, zero), ones_a], axis=1),
                    jnp.concatenate([jnp.where(first, zero, vp2), ones_b], axis=1),
                    jnp.concatenate([jnp.where(first, zero, vo2), ones_b], axis=1)], axis=0)
                r = _dot(e_all[p], v_cat)
                den = r[:, LANES:]
                step_o[p, pl.ds(off, blk), :] = r[:, :LANES] / den
                step_lse[p, pl.ds(off, blk), :] = m_all[p] + jnp.log(den)

        def body(i, carry):
            blocks = [ATT_BLOCKS_PER_TRIP * i + u for u in range(ATT_BLOCKS_PER_TRIP)]
            staged = [scores(a) for a in blocks]
            for a, (e_all, m_all) in zip(blocks, staged):
                values(a, e_all, m_all)
            return carry

        lax.fori_loop(0, ATT_STEP_BLOCKS // ATT_BLOCKS_PER_TRIP, body, 0)

    def rows_of(ref3, rows):
        return jnp.concatenate([ref3[c, rows, :] for c in range(ref3.shape[0])], axis=1)

    seen_tile = n > 0
    chained = lambda a: jnp.logical_or(a > 0, seen_tile)

    def keep_near(j, c):
        for rho in range(d_mid):
            src = pl.ds(rho, blk, stride=d_mid)
            dst = slice(rho * region + j * blk, rho * region + (j + 1) * blk)
            acc_o[c, dst, :] = step_o[c, src, :]
            acc_lse[c, dst, :] = step_lse[c, src, :]

    def merge_mid(rho, c):
        reg = slice(rho * region, (rho + 1) * region)
        o, lse = _merge_lse(acc_o[c, reg, :], acc_lse[c, reg, :], step_o[c], step_lse[c], True)
        acc_o[c, reg, :] = o
        acc_lse[c, reg, :] = lse

    def merge_far(rho, c):
        for a in range(ATT_STEP_BLOCKS):
            dst = pl.ds(rho * region + a, blk, stride=d_mid)
            blk_rows = slice(a * blk, (a + 1) * blk)
            o, _ = _merge_lse(acc_o[c, dst, :], acc_lse[c, dst, :], step_o[c, blk_rows, :], step_lse[c, blk_rows, :], False)
            acc_o[c, dst, :] = o
        o_ref[c, pl.ds(rho, region, stride=d_mid), :] = acc_o[c, rho * region:(rho + 1) * region, :]

    def later(fn, *args):
        pending.extend(functools.partial(fn, *args, c) for c in range(rep))

    kv_ext[0:blk, :] = carry_near[...]
    for j in range(ATT_TILE // ATT_STEP_ROWS):
        rows = slice(j * ATT_STEP_ROWS, (j + 1) * ATT_STEP_ROWS)
        project(0, rows_of(x_ref, rows).astype(BF16), cos_s[rows, :], sin_s[rows, :])
        run_blocks(kv_ext, 0, (lambda a: True) if j > 0 else chained)
        kv_ext[0:blk, :] = kv_ext[ATT_STEP_ROWS:ATT_STEP_ROWS + blk, :]
        later(keep_near, j)
    carry_near[...] = kv_ext[0:blk, :]

    for rho in range(d_mid):
        src = pl.ds(rho, region, stride=d_mid)
        for c in range(xg.shape[0]):
            xg[c] = x_ref[c, src, :]
        cos_g[...] = cos_s[src, :]
        sin_g[...] = sin_s[src, :]

        kv_ext[0:blk, :] = carry_mid[rho * blk:(rho + 1) * blk, :]
        project(1, rows_of(xg, slice(None)).astype(BF16), cos_g[...], sin_g[...])
        run_blocks(kv_ext, 0, chained)
        carry_mid[rho * blk:(rho + 1) * blk, :] = kv_ext[ATT_STEP_ROWS:ATT_STEP_ROWS + blk, :]
        later(merge_mid, rho)

        far = [pl.ds(a, blk, stride=d_mid) for a in range(ATT_STEP_BLOCKS)]
        project(2, jnp.concatenate([rows_of(xg, f) for f in far], axis=0).astype(BF16),
                jnp.concatenate([cos_g[f, :] for f in far], axis=0), jnp.concatenate([sin_g[f, :] for f in far], axis=0))
        run_blocks(carry_far, rho * ATT_STEP_ROWS, lambda a: seen_tile)
        carry_far[rho * ATT_STEP_ROWS:(rho + 1) * ATT_STEP_ROWS, :] = kv_ext[blk:blk + ATT_STEP_ROWS, :]
        later(merge_far, rho)
    flush(1)


def _attention(h_slabs, pos_rows, w_qkv, batch, seq):
    t, d = h_slabs.shape[1], w_qkv.shape[0]
    gw = ATT_GROUP_WIDTH
    assert seq % ATT_TILE == 0
    tiles = seq // ATT_TILE
    tile = lambda b, n: b * tiles + n
    kv_rows = ATT_BLOCK + ATT_STEP_ROWS
    d_mid, d_far = ATT_GROUPS[1][1], ATT_GROUPS[2][1]
    return pl.pallas_call(
        _attn_fused_kernel,
        grid=(batch, tiles),
        in_specs=[
            _slab_spec(ATT_TILE, d, tile),
            pl.BlockSpec((None, 1, ATT_TILE), lambda b, n: (tile(b, n), 0, 0)),
            _resident(w_qkv.shape),
        ],
        out_specs=_slab_spec(ATT_TILE, gw, tile),
        out_shape=jax.ShapeDtypeStruct((gw // LANES, t, LANES), F32),
        scratch_shapes=[
            pltpu.VMEM((ATT_TILE, LANES), F32),
            pltpu.VMEM((ATT_TILE, LANES), F32),
            pltpu.VMEM((d // LANES, ATT_STEP_ROWS, LANES), F32),
            pltpu.VMEM((ATT_STEP_ROWS, LANES), F32),
            pltpu.VMEM((ATT_STEP_ROWS, LANES), F32),
            pltpu.VMEM((ATT_STEP_ROWS, gw), BF16),
            pltpu.VMEM((kv_rows, 2 * gw), BF16),
            pltpu.VMEM((ATT_BLOCK, 2 * gw), BF16),
            pltpu.VMEM((d_mid * ATT_BLOCK, 2 * gw), BF16),
            pltpu.VMEM((d_far * ATT_BLOCK, 2 * gw), BF16),
            pltpu.VMEM((gw // LANES, ATT_STEP_ROWS, LANES), F32),
            pltpu.VMEM((gw // LANES, ATT_STEP_ROWS, LANES), F32),
            pltpu.VMEM((gw // LANES, ATT_TILE, LANES), F32),
            pltpu.VMEM((gw // LANES, ATT_TILE, LANES), F32),
        ],
        compiler_params=_params("parallel", "arbitrary"),
        name="attn_fused",
    )(h_slabs, pos_rows, w_qkv)


def _split2(x):
    hi = x.astype(BF16)
    lo = (x - hi.astype(F32)).astype(BF16)
    return hi, lo


def _rows_bcast(x, row_in_block):
    parts = [jnp.broadcast_to(x[i + row_in_block:i + row_in_block + 1, :], (SUBLANES, x.shape[1]))
             for i in range(0, x.shape[0], SUBLANES)]
    return jnp.concatenate(parts, axis=0)


def _hgrn_kernel(lbl_ref, ng_ref, x_ref, w_ref, o_ref, z_s, *st_refs, layer, n_chunks):
    c = HGRN_CHUNK
    width = HGRN_HEADS * HGRN_DIM
    blk = SUBLANES
    nblk = c // blk
    kinds = {"q": 0, "f": 1, "i": 2, "g": 3}

    @pl.when(pl.program_id(1) == 0)
    def _():
        for st_ref in st_refs:
            st_ref[...] = jnp.zeros_like(st_ref)

    lgt = lbl_ref[...]
    ex = jnp.exp(lgt - jnp.max(lgt, axis=0, keepdims=True))
    sm = ex / jnp.sum(ex, axis=0, keepdims=True)
    lb = jnp.sum(sm[0:layer + 1, :], axis=0, keepdims=True)

    r_i = lax.broadcasted_iota(jnp.int32, (c, c), 0)
    c_i = lax.broadcasted_iota(jnp.int32, (c, c), 1)
    tri = (c_i <= r_i).astype(BF16)
    rowid = lax.broadcasted_iota(jnp.int32, (c, 1), 0)
    second = {hs: (rowid & hs) != 0 for hs in (1, 2, 4)}
    pair = {hs: jnp.logical_and((r_i // (2 * hs)) == (c_i // (2 * hs)),
                                jnp.logical_and((r_i & hs) != 0, (c_i & hs) == 0)) for hs in (1, 2, 4)}
    diag = r_i == c_i
    ng = ng_ref[...]
    zeros_blk = jnp.zeros((blk, HGRN_DIM), F32)
    heads = [slice(h * HGRN_DIM, (h + 1) * HGRN_DIM) for h in range(HGRN_HEADS)]

    xb = _load_slabs(x_ref).astype(BF16)

    def project(kind):
        lo = kinds[kind] * width
        for j in range(width // PROJ_CHUNK):
            sl = slice(lo + j * PROJ_CHUNK, lo + (j + 1) * PROJ_CHUNK)
            z_s[:, sl] = _dot(xb, w_ref[:, sl].astype(BF16))

    def pre(kind, ci):
        lo = kinds[kind] * width
        return z_s[ci * c:(ci + 1) * c, lo:lo + width]

    def decay_part(ci):
        f = lb + (1.0 - lb) * jax.nn.sigmoid(pre("f", ci))
        hi, lo = _split2(jnp.log(f))
        b = _dot(tri, hi) + _dot(tri, lo)
        return f, b

    def chunk(ci, f, b):
        rows = slice(ci * c, (ci + 1) * c)
        k = 1.0 - f
        q = _silu(pre("q", ci))
        v = pre("i", ci)
        b_last = b[c - 1:c, :]
        qs = (q * jnp.exp(b)).astype(BF16)
        ks = (k * jnp.exp(b_last - b)).astype(BF16)
        decay = jnp.exp(b_last)
        qb, kb, vb = q.astype(BF16), k.astype(BF16), v.astype(BF16)

        q_far, k_far = [], []
        for j in range(nblk - 1):
            lo_r = (j + 1) * blk
            b_end = b[lo_r - 1:lo_r, :]
            q_far.append(q[lo_r:, :] * jnp.exp(b[lo_r:, :] - b_end))
            k_far.append(k[lo_r - blk:lo_r, :] * jnp.exp(b_end - b[lo_r - blk:lo_r, :]))

        lvl_q, lvl_k = {}, {}
        for hs, bref in ((4, _rows_bcast(b, 3)),
                         (2, jnp.where(second[4], _rows_bcast(b, 5), _rows_bcast(b, 1)))):
            x = jnp.exp(-jnp.abs(b - bref))
            lvl_q[hs] = jnp.where(second[hs], q * x, 0.0).astype(BF16)
            lvl_k[hs] = jnp.where(second[hs], 0.0, k * x).astype(BF16)
        lvl_q[1] = jnp.where(second[1], q * f, 0.0).astype(BF16)
        lvl_k[1] = jnp.where(second[1], 0.0, k).astype(BF16)

        o_inter, att_all = [], []
        for h, sl in enumerate(heads):
            st = st_refs[h][...]
            o_inter.append(_dot_nt(qs[:, sl], st.astype(BF16)))
            q_cat = jnp.concatenate(
                [jnp.concatenate([jnp.zeros(((j + 1) * blk, HGRN_DIM), F32), q_far[j][:, sl]], axis=0)
                 for j in range(nblk - 1)], axis=1).astype(BF16)
            k_cat = jnp.concatenate(
                [jnp.concatenate([zeros_blk] * j + [k_far[j][:, sl]] + [zeros_blk] * (nblk - 1 - j), axis=0)
                 for j in range(nblk - 1)], axis=1).astype(BF16)
            att = _dot_nt(q_cat, k_cat)
            for hs in (4, 2, 1):
                att = att + jnp.where(pair[hs], _dot_nt(lvl_q[hs][:, sl], lvl_k[hs][:, sl]), 0.0)
            att = att + jnp.where(diag, _dot_nt(qb[:, sl], kb[:, sl]), 0.0)
            att_all.append(att.astype(BF16))
            st_refs[h][...] = st * decay[:, sl] + _dot_tn(vb[:, sl], ks[:, sl])
        g_act = _silu(pre("g", ci))
        for h, sl in enumerate(heads):
            o = o_inter[h] + _dot(att_all[h], vb[:, sl])
            o = o * lax.rsqrt(jnp.mean(o * o, axis=1, keepdims=True) + RMS_EPS)
            o_ref[rows, sl] = (o * ng[:, sl] * g_act[:, sl]).astype(o_ref.dtype)

    project("f")
    decays = [decay_part(ci) for ci in range(n_chunks)]
    project("q")
    project("i")
    project("g")
    for ci, (f, b) in enumerate(decays):
        chunk(ci, f, b)


def _hgrn_scan(h_slabs, w_hgrn, lb_logits, norm_g, layer, batch, seq):
    width = HGRN_HEADS * HGRN_DIM
    d = w_hgrn.shape[0]
    ts = min(HGRN_STEP_TOKENS, seq)
    steps = seq // ts
    body = functools.partial(_hgrn_kernel, layer=layer, n_chunks=ts // HGRN_CHUNK)
    return pl.pallas_call(
        body,
        grid=(batch, steps),
        in_specs=[
            _resident(lb_logits.shape),
            _resident(norm_g.shape),
            _slab_spec(ts, d, lambda b, s: b * steps + s),
            _resident(w_hgrn.shape),
        ],
        out_specs=pl.BlockSpec((ts, width), lambda b, s: (b * steps + s, 0)),
        out_shape=jax.ShapeDtypeStruct((batch * seq, width), BF16),
        scratch_shapes=[pltpu.VMEM((ts, w_hgrn.shape[1]), F32)] + [pltpu.VMEM((HGRN_DIM, HGRN_DIM), F32)] * HGRN_HEADS,
        compiler_params=_params("parallel", "arbitrary"),
        name="hgrn_scan",
    )(lb_logits, norm_g, h_slabs, w_hgrn)


def _merge_kernel(h_ref, oa_ref, oh_ref, wg_ref, wpa_ref, wpb_ref, wo_ref, g_ref, b_ref, out_ref, *, alpha):
    h = _load_slabs(h_ref)
    d = h.shape[1]
    hb = h.astype(BF16)
    w = lambda ref, cols=slice(None): ref[:, cols].astype(BF16)
    y_att = _dot(_load_slabs(oa_ref).astype(BF16), w(wpa_ref))
    y_hgrn = _dot(oh_ref[...], w(wpb_ref))
    merged = (jax.nn.sigmoid(_dot(hb, w(wg_ref, slice(0, d)))) * y_att
              + jax.nn.sigmoid(_dot(hb, w(wg_ref, slice(d, 2 * d)))) * y_hgrn)
    mix = _dot(merged.astype(BF16), w(wo_ref))
    out_ref[...] = _layer_norm(alpha * h + mix, g_ref[...], b_ref[...])


def _merge(h_slabs, o_att, o_hgrn, w_gate, w_pa, w_pb, w_o, g, b, alpha):
    t, d = o_hgrn.shape
    tm = min(TOKEN_TILE, t)
    row = lambda i: (i, 0)
    in_specs = [_slab_spec(tm, d, lambda i: i),
                _slab_spec(tm, o_att.shape[0] * LANES, lambda i: i),
                pl.BlockSpec((tm, d), row)]
    in_specs += [_resident(a.shape) for a in (w_gate, w_pa, w_pb, w_o, g, b)]
    return pl.pallas_call(
        functools.partial(_merge_kernel, alpha=alpha),
        grid=(t // tm,),
        in_specs=in_specs,
        out_specs=pl.BlockSpec((tm, d), row),
        out_shape=jax.ShapeDtypeStruct((t, d), F32),
        compiler_params=_params("parallel"),
        name="merge_ln",
    )(h_slabs, o_att, o_hgrn, w_gate, w_pa, w_pb, w_o, g, b)


def kernel(x, p, positions, ffn1_w_in, ffn1_w_out, ln1_g, ln1_b, w_mix_in, hgrn_lb_logits, hgrn_norm_g, w_proj_attn, w_proj_hgrn, w_mix_out, ln2_g, ln2_b, ffn2_w_in, ffn2_w_out, ln3_g, ln3_b, ple_w_proj, ple_w_gate):
    batch, seq, d_model = x.shape
    depth = ffn1_w_in.shape[0]
    alpha = (2 * depth) ** 0.25
    width = HGRN_HEADS * HGRN_DIM
    gw = ATT_GROUP_WIDTH
    t = batch * seq
    h = x.reshape(t, d_model)
    pos_rows = positions.reshape(t // ATT_TILE, 1, ATT_TILE)
    lb_logits = hgrn_lb_logits.astype(F32)
    vec = lambda a: a.reshape(1, -1).astype(F32)
    c_hgrn = 3 * ATT_WIDTH
    c_gate = c_hgrn + 4 * width
    pair_order, _ = _pair_lane_order()
    att_cols = []
    for g in range(len(ATT_GROUPS)):
        for part in range(3):
            lo = part * ATT_WIDTH + g * gw
            slab_order = pair_order if part < 2 else list(range(LANES))
            att_cols += [lo + p * LANES + src for p in range(gw // LANES) for src in slab_order]
    att_cols = jnp.asarray(att_cols, jnp.int32)
    for i in range(depth):
        w_in = w_mix_in[i]
        w_qkv = w_in[:, :c_hgrn].astype(BF16)[:, att_cols]
        w_hgrn = w_in[:, c_hgrn:c_gate].astype(F32)
        w_gate = w_in[:, c_gate:].astype(F32)

        h = _ffn_ln(h, ffn1_w_in[i].astype(F32), ffn1_w_out[i].astype(BF16), vec(ln1_g[i]), vec(ln1_b[i]), alpha)
        for window, dilation in ATT_GROUPS:
            assert window == ATT_BLOCK * dilation
        o_att = _attention(h, pos_rows, w_qkv, batch, seq)
        o_hgrn = _hgrn_scan(h, w_hgrn, lb_logits, vec(hgrn_norm_g[i]), i, batch, seq)
        h = _merge(h, o_att, o_hgrn, w_gate, w_proj_attn[i].astype(F32), w_proj_hgrn[i].astype(F32),
                   w_mix_out[i].astype(F32), vec(ln2_g[i]), vec(ln2_b[i]), alpha)
        h = _ffn_ln(h, ffn2_w_in[i].astype(F32), ffn2_w_out[i].astype(BF16), vec(ln3_g[i]), vec(ln3_b[i]), alpha,
                    ple=(p[i].reshape(t, -1), ple_w_gate[i].astype(BF16), ple_w_proj[i].astype(BF16)))
    return h.reshape(batch, seq, d_model)
```

```python
import functools
import math

import jax
import jax.numpy as jnp
from jax import lax
from jax.experimental import pallas as pl
from jax.experimental.pallas import tpu as pltpu

F32 = jnp.float32
BF16 = jnp.bfloat16

ATT_HEAD_DIM = 64
ATT_HEADS_PER_GROUP = 8
ATT_GROUPS = ((128, 1), (512, 4), (2048, 16))
ATT_GROUP_WIDTH = ATT_HEADS_PER_GROUP * ATT_HEAD_DIM
ATT_WIDTH = ATT_GROUP_WIDTH * len(ATT_GROUPS)
ATT_BLOCK = 128
ROPE_THETA = 500000.0
ROPE_DIM = ATT_HEAD_DIM // 4
HGRN_HEADS = 8
HGRN_DIM = 128
LN_EPS = 1e-5
RMS_EPS = 1e-6

LANES = 128
SUBLANES = 8
VMEM_LIMIT_BYTES = 62 * 1024 * 1024

TOKEN_TILE = 1024
FF_CHUNK = 256
PROJ_CHUNK = 512
ATT_STEP_BLOCKS = 4
ATT_STEP_ROWS = ATT_STEP_BLOCKS * ATT_BLOCK
ATT_BLOCKS_PER_TRIP = 2
ATT_TILE = ATT_BLOCK * max(d for _, d in ATT_GROUPS)
HGRN_CHUNK = 64
HGRN_STEP_TOKENS = 512


def _params(*sem):
    return pltpu.CompilerParams(dimension_semantics=sem, vmem_limit_bytes=VMEM_LIMIT_BYTES)


def _resident(shape):
    nd = len(shape)
    return pl.BlockSpec(shape, lambda *_: (0,) * nd)


def _layer_norm(y, g, b):
    mu = jnp.mean(y, axis=-1, keepdims=True)
    yc = y - mu
    var = jnp.mean(yc * yc, axis=-1, keepdims=True)
    return yc * lax.rsqrt(var + LN_EPS) * g + b


def _silu(a):
    return a * jax.nn.sigmoid(a)


def _slab_spec(rows, d, index_map, **kw):
    return pl.BlockSpec((d // LANES, rows, LANES), lambda *i: (0, index_map(*i), 0), **kw)


def _load_slabs(ref, rows=slice(None)):
    return jnp.concatenate([ref[c, rows, :] for c in range(ref.shape[0])], axis=1)


def _store_slabs(ref, val, rows=slice(None)):
    for c in range(ref.shape[0]):
        ref[c, rows, :] = val[:, c * LANES:(c + 1) * LANES]


def _dot(a, b):
    return jnp.dot(a, b, preferred_element_type=F32)


def _dot_nt(a, b):
    return lax.dot_general(a, b, (((1,), (1,)), ((), ())), preferred_element_type=F32)


def _dot_tn(a, b):
    return lax.dot_general(a, b, (((0,), (0,)), ((), ())), preferred_element_type=F32)


def _ffn_core(x, win_ref, wout_ref, d_ff):
    xb = x.astype(BF16)
    acc = jnp.zeros(x.shape, F32)
    for lo in range(0, d_ff, FF_CHUNK):
        hi = min(lo + FF_CHUNK, d_ff)
        a = _dot(xb, win_ref[:, lo:hi].astype(BF16))
        u = _dot(xb, win_ref[:, d_ff + lo:d_ff + hi].astype(BF16))
        acc = acc + _dot((_silu(a) * u).astype(BF16), wout_ref[lo:hi, :])
    return acc


def _ffn_ln_kernel(x_ref, win_ref, wout_ref, g_ref, b_ref, o_ref, *, d_ff, alpha):
    x = x_ref[...]
    y = alpha * x + 0.5 * _ffn_core(x, win_ref, wout_ref, d_ff)
    _store_slabs(o_ref, _layer_norm(y, g_ref[...], b_ref[...]))


def _ffn_ln_ple_kernel(x_ref, p_ref, win_ref, wout_ref, g_ref, b_ref, wpg_ref, wpp_ref, o_ref, *, d_ff, alpha):
    x = x_ref[...]
    y = alpha * x + 0.5 * _ffn_core(x, win_ref, wout_ref, d_ff)
    h = _layer_norm(y, g_ref[...], b_ref[...])
    gate = jax.nn.sigmoid(_dot(h.astype(BF16), wpg_ref[...]))
    emb = _dot(p_ref[...].astype(BF16), wpp_ref[...])
    o_ref[...] = h + gate * emb


def _ffn_ln(x2d, w_in, w_out, g, b, alpha, ple=None):
    t, d = x2d.shape
    d_ff = w_out.shape[0]
    tm = min(TOKEN_TILE, t)
    row = lambda i: (i, 0)
    in_specs = [pl.BlockSpec((tm, d), row)]
    args = [x2d]
    if ple is not None:
        p2d, w_pg, w_pp = ple
        in_specs.append(pl.BlockSpec((tm, p2d.shape[1]), row))
        args.append(p2d)
    in_specs += [_resident(w_in.shape), _resident(w_out.shape), _resident(g.shape), _resident(b.shape)]
    args += [w_in, w_out, g, b]
    if ple is None:
        body = functools.partial(_ffn_ln_kernel, d_ff=d_ff, alpha=alpha)
    else:
        in_specs += [_resident(w_pg.shape), _resident(w_pp.shape)]
        args += [w_pg, w_pp]
        body = functools.partial(_ffn_ln_ple_kernel, d_ff=d_ff, alpha=alpha)
    return pl.pallas_call(
        body,
        grid=(t // tm,),
        in_specs=in_specs,
        out_specs=_slab_spec(tm, d, lambda i: i) if ple is None else pl.BlockSpec((tm, d), row),
        out_shape=jax.ShapeDtypeStruct((d // LANES, t, LANES) if ple is None else (t, d), F32),
        compiler_params=_params("parallel"),
        name="ffn_ln" if ple is None else "ffn_ln_ple",
    )(*args)


def _pair_lane_order():
    hd, half = ATT_HEAD_DIM, ROPE_DIM // 2
    a, b = list(range(hd)), list(range(hd, 2 * hd))
    order = (a[:half] + b[:half] + a[ROPE_DIM:] + a[half:ROPE_DIM] + b[half:ROPE_DIM] + b[ROPE_DIM:])
    return order, [src < hd for src in order]


def _rope_tables(pos_row_f32):
    half = ROPE_DIM // 2
    assert half == SUBLANES
    freq = lax.broadcasted_iota(jnp.int32, (half, 1), 0).astype(F32)
    ang = jnp.exp(-math.log(ROPE_THETA) * freq * (2.0 / ROPE_DIM)) * pos_row_f32
    k_row = lax.broadcasted_iota(jnp.int32, (4 * half, LANES), 0)
    k_lane = lax.broadcasted_iota(jnp.int32, (4 * half, LANES), 1)
    take = jnp.logical_and((k_row & (half - 1)) == (k_lane & (half - 1)), k_row < 3 * half)
    spread_matrix = jnp.where(take, 1.0, 0.0).astype(BF16)

    def spread(c8):
        hi = c8.astype(BF16).astype(F32)
        mid = (c8 - hi).astype(BF16).astype(F32)
        lo = c8 - hi - mid
        terms = jnp.concatenate([hi, mid, lo, jnp.zeros_like(hi)], axis=0).astype(BF16)
        return _dot_tn(terms, spread_matrix)

    lane = lax.broadcasted_iota(jnp.int32, (1, LANES), 1)
    rotary = (lane & (ATT_HEAD_DIM - 1)) < ROPE_DIM
    sign = jnp.where(lane < ATT_HEAD_DIM, -1.0, 1.0)
    return jnp.where(rotary, spread(jnp.cos(ang)), 1.0), jnp.where(rotary, spread(jnp.sin(ang)) * sign, 0.0)


def _merge_lse(o_a, lse_a, o_b, lse_b, want_lse):
    m = jnp.maximum(lse_a, lse_b)
    w_a = jnp.exp(lse_a - m)
    w_b = jnp.exp(lse_b - m)
    den = w_a + w_b
    o = (w_a * o_a + w_b * o_b) / den
    return o, (m + jnp.log(den) if want_lse else None)


def _attn_fused_kernel(x_ref, pos_ref, w_ref, o_ref, cos_s, sin_s, xg, cos_g, sin_g, q_buf, kv_ext,
                       carry_near, carry_mid, carry_far, step_o, step_lse, acc_o, acc_lse):
    n = pl.program_id(1)
    blk, gw = ATT_BLOCK, ATT_GROUP_WIDTH
    (_, d_near), (_, d_mid), (_, d_far) = ATT_GROUPS
    region = ATT_TILE // d_mid
    assert d_near == 1 and d_far == d_mid * d_mid and region == ATT_STEP_ROWS and d_mid == ATT_STEP_BLOCKS

    @pl.when(n == 0)
    def _():
        carry_near[...] = jnp.zeros_like(carry_near)
        carry_mid[...] = jnp.zeros_like(carry_mid)
        carry_far[...] = jnp.zeros_like(carry_far)

    cos_s[...], sin_s[...] = _rope_tables(pos_ref[...].astype(F32))

    row = lax.broadcasted_iota(jnp.int32, (blk, blk), 0)
    col = lax.broadcasted_iota(jnp.int32, (blk, blk), 1)
    neg_inf = jnp.float32(-jnp.inf)
    bias_own = jnp.where(col <= row, 0.0, neg_inf)
    bias_prev = jnp.where(col >= row, 0.0, neg_inf)
    lane = lax.broadcasted_iota(jnp.int32, (1, LANES), 1)
    first = lane < ATT_HEAD_DIM
    lane_j = lane & (ATT_HEAD_DIM - 1)
    half = ROPE_DIM // 2
    first_qk = jnp.logical_or(lane_j < half, jnp.logical_and(lane < ATT_HEAD_DIM, lane_j >= ROPE_DIM))
    ones_a = jnp.broadcast_to(jnp.where(first, 1.0, 0.0).astype(BF16), (blk, LANES))
    ones_b = jnp.broadcast_to(jnp.where(first, 0.0, 1.0).astype(BF16), (blk, LANES))
    rep = gw // LANES

    pending = []

    def flush(parts_left):
        for _ in range(-(-len(pending) // parts_left)):
            pending.pop(0)()

    def project(group, xb, cos, sin):
        for part in range(3):
            lo = (3 * group + part) * gw
            z = _dot(xb, w_ref[:, lo:lo + gw])
            flush(3 - part)
            if part < 2:
                slabs = []
                for c in range(rep):
                    zc = z[:, c * LANES:(c + 1) * LANES]
                    slabs.append(zc * cos + pltpu.roll(zc, ATT_HEAD_DIM, 1) * sin)
                z = jnp.concatenate(slabs, axis=1)
            if part == 0:
                q_buf[...] = (z * ATT_HEAD_DIM ** -0.5).astype(BF16)
            else:
                kv_ext[blk:blk + ATT_STEP_ROWS, (part - 1) * gw:part * gw] = z.astype(BF16)

    def run_blocks(prev_ref, prev_base, prev_valid):
        def scores(a):
            off = pl.multiple_of(a * blk, blk)
            q = q_buf[pl.ds(off, blk), :]
            own = kv_ext[pl.ds(off + blk, blk), 0:gw]
            prev = prev_ref[pl.ds(prev_base + off, blk), 0:gw]
            bias = jnp.concatenate([jnp.where(prev_valid(a), bias_prev, neg_inf), bias_own], axis=1)
            zero = jnp.zeros((blk, LANES), BF16)
            e_all, m_all = [], []
            for p in range(rep):
                sl = slice(p * LANES, (p + 1) * LANES)
                q2 = q[:, sl]
                q_cat = jnp.concatenate([jnp.where(first_qk, q2, zero), jnp.where(first_qk, zero, q2)], axis=0)
                k_cat = jnp.concatenate([prev[:, sl], own[:, sl]], axis=0)
                s = _dot_nt(q_cat, k_cat)
                e_parts, m_parts = [], []
                for hd in range(2):
                    s_h = s[hd * blk:(hd + 1) * blk, :] + bias
                    m = jnp.max(s_h, axis=1, keepdims=True)
                    e_parts.append(jnp.exp(s_h - m).astype(BF16))
                    m_parts.append(m)
                e_all.append(jnp.concatenate(e_parts, axis=1))
                m_all.append(jnp.where(first, m_parts[0], m_parts[1]))
            return e_all, m_all

        def values(a, e_all, m_all):
            off = pl.multiple_of(a * blk, blk)
            own = kv_ext[pl.ds(off + blk, blk), gw:2 * gw]
            prev = prev_ref[pl.ds(prev_base + off, blk), gw:2 * gw]
            zero = jnp.zeros((blk, LANES), BF16)
            for p in range(rep):
                sl = slice(p * LANES, (p + 1) * LANES)
                vp2, vo2 = prev[:, sl], own[:, sl]
                v_cat = jnp.concatenate([
                    jnp.concatenate([jnp.where(first, vp2, zero), ones_a], axis=1),
                    jnp.concatenate([jnp.where(first, vo2, zero), ones_a], axis=1),
                    jnp.concatenate([jnp.where(first, zero, vp2), ones_b], axis=1),
                    jnp.concatenate([jnp.where(first, zero, vo2), ones_b], axis=1)], axis=0)
                r = _dot(e_all[p], v_cat)
                den = r[:, LANES:]
                step_o[p, pl.ds(off, blk), :] = r[:, :LANES] / den
                step_lse[p, pl.ds(off, blk), :] = m_all[p] + jnp.log(den)

        def body(i, carry):
            blocks = [ATT_BLOCKS_PER_TRIP * i + u for u in range(ATT_BLOCKS_PER_TRIP)]
            staged = [scores(a) for a in blocks]
            for a, (e_all, m_all) in zip(blocks, staged):
                values(a, e_all, m_all)
            return carry

        lax.fori_loop(0, ATT_STEP_BLOCKS // ATT_BLOCKS_PER_TRIP, body, 0)

    def rows_of(ref3, rows):
        return jnp.concatenate([ref3[c, rows, :] for c in range(ref3.shape[0])], axis=1)

    seen_tile = n > 0
    chained = lambda a: jnp.logical_or(a > 0, seen_tile)

    def keep_near(j, c):
        for rho in range(d_mid):
            src = pl.ds(rho, blk, stride=d_mid)
            dst = slice(rho * region + j * blk, rho * region + (j + 1) * blk)
            acc_o[c, dst, :] = step_o[c, src, :]
            acc_lse[c, dst, :] = step_lse[c, src, :]

    def merge_mid(rho, c):
        reg = slice(rho * region, (rho + 1) * region)
        o, lse = _merge_lse(acc_o[c, reg, :], acc_lse[c, reg, :], step_o[c], step_lse[c], True)
        acc_o[c, reg, :] = o
        acc_lse[c, reg, :] = lse

    def merge_far(rho, c):
        for a in range(ATT_STEP_BLOCKS):
            dst = pl.ds(rho * region + a, blk, stride=d_mid)
            blk_rows = slice(a * blk, (a + 1) * blk)
            o, _ = _merge_lse(acc_o[c, dst, :], acc_lse[c, dst, :], step_o[c, blk_rows, :], step_lse[c, blk_rows, :], False)
            acc_o[c, dst, :] = o
        o_ref[c, pl.ds(rho, region, stride=d_mid), :] = acc_o[c, rho * region:(rho + 1) * region, :]

    def later(fn, *args):
        pending.extend(functools.partial(fn, *args, c) for c in range(rep))

    kv_ext[0:blk, :] = carry_near[...]
    for j in range(ATT_TILE // ATT_STEP_ROWS):
        rows = slice(j * ATT_STEP_ROWS, (j + 1) * ATT_STEP_ROWS)
        project(0, rows_of(x_ref, rows).astype(BF16), cos_s[rows, :], sin_s[rows, :])
        run_blocks(kv_ext, 0, (lambda a: True) if j > 0 else chained)
        kv_ext[0:blk, :] = kv_ext[ATT_STEP_ROWS:ATT_STEP_ROWS + blk, :]
        later(keep_near, j)
    carry_near[...] = kv_ext[0:blk, :]

    for rho in range(d_mid):
        src = pl.ds(rho, region, stride=d_mid)
        for c in range(xg.shape[0]):
            xg[c] = x_ref[c, src, :]
        cos_g[...] = cos_s[src, :]
        sin_g[...] = sin_s[src, :]

        kv_ext[0:blk, :] = carry_mid[rho * blk:(rho + 1) * blk, :]
        project(1, rows_of(xg, slice(None)).astype(BF16), cos_g[...], sin_g[...])
        run_blocks(kv_ext, 0, chained)
        carry_mid[rho * blk:(rho + 1) * blk, :] = kv_ext[ATT_STEP_ROWS:ATT_STEP_ROWS + blk, :]
        later(merge_mid, rho)

        far = [pl.ds(a, blk, stride=d_mid) for a in range(ATT_STEP_BLOCKS)]
        project(2, jnp.concatenate([rows_of(xg, f) for f in far], axis=0).astype(BF16),
                jnp.concatenate([cos_g[f, :] for f in far], axis=0), jnp.concatenate([sin_g[f, :] for f in far], axis=0))
        run_blocks(carry_far, rho * ATT_STEP_ROWS, lambda a: seen_tile)
        carry_far[rho * ATT_STEP_ROWS:(rho + 1) * ATT_STEP_ROWS, :] = kv_ext[blk:blk + ATT_STEP_ROWS, :]
        later(merge_far, rho)
    flush(1)


def _attention(h_slabs, pos_rows, w_qkv, batch, seq):
    t, d = h_slabs.shape[1], w_qkv.shape[0]
    gw = ATT_GROUP_WIDTH
    assert seq % ATT_TILE == 0
    tiles = seq // ATT_TILE
    tile = lambda b, n: b * tiles + n
    kv_rows = ATT_BLOCK + ATT_STEP_ROWS
    d_mid, d_far = ATT_GROUPS[1][1], ATT_GROUPS[2][1]
    return pl.pallas_call(
        _attn_fused_kernel,
        grid=(batch, tiles),
        in_specs=[
            _slab_spec(ATT_TILE, d, tile),
            pl.BlockSpec((None, 1, ATT_TILE), lambda b, n: (tile(b, n), 0, 0)),
            _resident(w_qkv.shape),
        ],
        out_specs=_slab_spec(ATT_TILE, gw, tile),
        out_shape=jax.ShapeDtypeStruct((gw // LANES, t, LANES), F32),
        scratch_shapes=[
            pltpu.VMEM((ATT_TILE, LANES), F32),
            pltpu.VMEM((ATT_TILE, LANES), F32),
            pltpu.VMEM((d // LANES, ATT_STEP_ROWS, LANES), F32),
            pltpu.VMEM((ATT_STEP_ROWS, LANES), F32),
            pltpu.VMEM((ATT_STEP_ROWS, LANES), F32),
            pltpu.VMEM((ATT_STEP_ROWS, gw), BF16),
            pltpu.VMEM((kv_rows, 2 * gw), BF16),
            pltpu.VMEM((ATT_BLOCK, 2 * gw), BF16),
            pltpu.VMEM((d_mid * ATT_BLOCK, 2 * gw), BF16),
            pltpu.VMEM((d_far * ATT_BLOCK, 2 * gw), BF16),
            pltpu.VMEM((gw // LANES, ATT_STEP_ROWS, LANES), F32),
            pltpu.VMEM((gw // LANES, ATT_STEP_ROWS, LANES), F32),
            pltpu.VMEM((gw // LANES, ATT_TILE, LANES), F32),
            pltpu.VMEM((gw // LANES, ATT_TILE, LANES), F32),
        ],
        compiler_params=_params("parallel", "arbitrary"),
        name="attn_fused",
    )(h_slabs, pos_rows, w_qkv)


def _split2(x):
    hi = x.astype(BF16)
    lo = (x - hi.astype(F32)).astype(BF16)
    return hi, lo


def _rows_bcast(x, row_in_block):
    parts = [jnp.broadcast_to(x[i + row_in_block:i + row_in_block + 1, :], (SUBLANES, x.shape[1]))
             for i in range(0, x.shape[0], SUBLANES)]
    return jnp.concatenate(parts, axis=0)


def _hgrn_kernel(lbl_ref, ng_ref, x_ref, w_ref, o_ref, z_s, *st_refs, layer, n_chunks):
    c = HGRN_CHUNK
    width = HGRN_HEADS * HGRN_DIM
    blk = SUBLANES
    nblk = c // blk
    kinds = {"q": 0, "f": 1, "i": 2, "g": 3}

    @pl.when(pl.program_id(1) == 0)
    def _():
        for st_ref in st_refs:
            st_ref[...] = jnp.zeros_like(st_ref)

    lgt = lbl_ref[...]
    ex = jnp.exp(lgt - jnp.max(lgt, axis=0, keepdims=True))
    sm = ex / jnp.sum(ex, axis=0, keepdims=True)
    lb = jnp.sum(sm[0:layer + 1, :], axis=0, keepdims=True)

    r_i = lax.broadcasted_iota(jnp.int32, (c, c), 0)
    c_i = lax.broadcasted_iota(jnp.int32, (c, c), 1)
    tri = (c_i <= r_i).astype(BF16)
    rowid = lax.broadcasted_iota(jnp.int32, (c, 1), 0)
    second = {hs: (rowid & hs) != 0 for hs in (1, 2, 4)}
    pair = {hs: jnp.logical_and((r_i // (2 * hs)) == (c_i // (2 * hs)),
                                jnp.logical_and((r_i & hs) != 0, (c_i & hs) == 0)) for hs in (1, 2, 4)}
    diag = r_i == c_i
    ng = ng_ref[...]
    zeros_blk = jnp.zeros((blk, HGRN_DIM), F32)
    heads = [slice(h * HGRN_DIM, (h + 1) * HGRN_DIM) for h in range(HGRN_HEADS)]

    xb = _load_slabs(x_ref).astype(BF16)

    def project(kind):
        lo = kinds[kind] * width
        for j in range(width // PROJ_CHUNK):
            sl = slice(lo + j * PROJ_CHUNK, lo + (j + 1) * PROJ_CHUNK)
            z_s[:, sl] = _dot(xb, w_ref[:, sl].astype(BF16))

    def pre(kind, ci):
        lo = kinds[kind] * width
        return z_s[ci * c:(ci + 1) * c, lo:lo + width]

    def decay_part(ci):
        f = lb + (1.0 - lb) * jax.nn.sigmoid(pre("f", ci))
        hi, lo = _split2(jnp.log(f))
        b = _dot(tri, hi) + _dot(tri, lo)
        return f, b

    def chunk(ci, f, b):
        rows = slice(ci * c, (ci + 1) * c)
        k = 1.0 - f
        q = _silu(pre("q", ci))
        v = pre("i", ci)
        b_last = b[c - 1:c, :]
        qs = (q * jnp.exp(b)).astype(BF16)
        ks = (k * jnp.exp(b_last - b)).astype(BF16)
        decay = jnp.exp(b_last)
        qb, kb, vb = q.astype(BF16), k.astype(BF16), v.astype(BF16)

        q_far, k_far = [], []
        for j in range(nblk - 1):
            lo_r = (j + 1) * blk
            b_end = b[lo_r - 1:lo_r, :]
            q_far.append(q[lo_r:, :] * jnp.exp(b[lo_r:, :] - b_end))
            k_far.append(k[lo_r - blk:lo_r, :] * jnp.exp(b_end - b[lo_r - blk:lo_r, :]))

        lvl_q, lvl_k = {}, {}
        for hs, bref in ((4, _rows_bcast(b, 3)),
                         (2, jnp.where(second[4], _rows_bcast(b, 5), _rows_bcast(b, 1)))):
            x = jnp.exp(-jnp.abs(b - bref))
            lvl_q[hs] = jnp.where(second[hs], q * x, 0.0).astype(BF16)
            lvl_k[hs] = jnp.where(second[hs], 0.0, k * x).astype(BF16)
        lvl_q[1] = jnp.where(second[1], q * f, 0.0).astype(BF16)
        lvl_k[1] = jnp.where(second[1], 0.0, k).astype(BF16)

        o_inter, att_all = [], []
        for h, sl in enumerate(heads):
            st = st_refs[h][...]
            o_inter.append(_dot_nt(qs[:, sl], st.astype(BF16)))
            q_cat = jnp.concatenate(
                [jnp.concatenate([jnp.zeros(((j + 1) * blk, HGRN_DIM), F32), q_far[j][:, sl]], axis=0)
                 for j in range(nblk - 1)], axis=1).astype(BF16)
            k_cat = jnp.concatenate(
                [jnp.concatenate([zeros_blk] * j + [k_far[j][:, sl]] + [zeros_blk] * (nblk - 1 - j), axis=0)
                 for j in range(nblk - 1)], axis=1).astype(BF16)
            att = _dot_nt(q_cat, k_cat)
            for hs in (4, 2, 1):
                att = att + jnp.where(pair[hs], _dot_nt(lvl_q[hs][:, sl], lvl_k[hs][:, sl]), 0.0)
            att = att + jnp.where(diag, _dot_nt(qb[:, sl], kb[:, sl]), 0.0)
            att_all.append(att.astype(BF16))
            st_refs[h][...] = st * decay[:, sl] + _dot_tn(vb[:, sl], ks[:, sl])
        g_act = _silu(pre("g", ci))
        for h, sl in enumerate(heads):
            o = o_inter[h] + _dot(att_all[h], vb[:, sl])
            o = o * lax.rsqrt(jnp.mean(o * o, axis=1, keepdims=True) + RMS_EPS)
            o_ref[rows, sl] = (o * ng[:, sl] * g_act[:, sl]).astype(o_ref.dtype)

    project("f")
    decays = [decay_part(ci) for ci in range(n_chunks)]
    project("q")
    project("i")
    project("g")
    for ci, (f, b) in enumerate(decays):
        chunk(ci, f, b)


def _hgrn_scan(h_slabs, w_mix, col0, lb_logits, norm_g, layer, batch, seq):
    width = HGRN_HEADS * HGRN_DIM
    d = w_mix.shape[1]
    assert col0 >= 4 * width and 2 * col0 <= w_mix.shape[2]
    ts = min(HGRN_STEP_TOKENS, seq)
    steps = seq // ts
    body = functools.partial(_hgrn_kernel, layer=layer, n_chunks=ts // HGRN_CHUNK)
    return pl.pallas_call(
        body,
        grid=(batch, steps),
        in_specs=[
            _resident(lb_logits.shape),
            _resident(norm_g.shape),
            _slab_spec(ts, d, lambda b, s: b * steps + s),
            pl.BlockSpec((None, d, col0), lambda b, s: (layer, 0, 1)),
        ],
        out_specs=pl.BlockSpec((ts, width), lambda b, s: (b * steps + s, 0)),
        out_shape=jax.ShapeDtypeStruct((batch * seq, width), BF16),
        scratch_shapes=[pltpu.VMEM((ts, 4 * width), F32)] + [pltpu.VMEM((HGRN_DIM, HGRN_DIM), F32)] * HGRN_HEADS,
        compiler_params=_params("parallel", "arbitrary"),
        name="hgrn_scan",
    )(lb_logits, norm_g, h_slabs, w_mix)


def _merge_kernel(h_ref, oa_ref, oh_ref, wga0, wga1, wgb0, wgb1, wpa_ref, wpb_ref, wo_ref, g_ref, b_ref, out_ref, *, alpha):
    h = _load_slabs(h_ref)
    d = h.shape[1]
    hb = h.astype(BF16)
    w = lambda ref: ref[...].astype(BF16)
    gate = lambda lo, hi: jax.nn.sigmoid(jnp.concatenate([_dot(hb, w(lo)), _dot(hb, w(hi))], axis=1))
    y_att = _dot(_load_slabs(oa_ref).astype(BF16), w(wpa_ref))
    y_hgrn = _dot(oh_ref[...], w(wpb_ref))
    merged = gate(wga0, wga1) * y_att + gate(wgb0, wgb1) * y_hgrn
    mix = _dot(merged.astype(BF16), w(wo_ref))
    out_ref[...] = _layer_norm(alpha * h + mix, g_ref[...], b_ref[...])


def _merge(h_slabs, o_att, o_hgrn, w_mix, gate_col, layer, w_pa, w_pb, w_o, g, b, alpha):
    t, d = o_hgrn.shape
    half = d // 2
    assert gate_col % half == 0
    tm = min(TOKEN_TILE, t)
    row = lambda i: (i, 0)
    in_specs = [_slab_spec(tm, d, lambda i: i),
                _slab_spec(tm, o_att.shape[0] * LANES, lambda i: i),
                pl.BlockSpec((tm, d), row)]
    in_specs += [pl.BlockSpec((None, d, half), lambda i, j=j: (layer, 0, gate_col // half + j)) for j in range(4)]
    in_specs += [_resident(a.shape) for a in (w_pa, w_pb, w_o, g, b)]
    return pl.pallas_call(
        functools.partial(_merge_kernel, alpha=alpha),
        grid=(t // tm,),
        in_specs=in_specs,
        out_specs=pl.BlockSpec((tm, d), row),
        out_shape=jax.ShapeDtypeStruct((t, d), F32),
        compiler_params=_params("parallel"),
        name="merge_ln",
    )(h_slabs, o_att, o_hgrn, w_mix, w_mix, w_mix, w_mix, w_pa, w_pb, w_o, g, b)


def kernel(x, p, positions, ffn1_w_in, ffn1_w_out, ln1_g, ln1_b, w_mix_in, hgrn_lb_logits, hgrn_norm_g, w_proj_attn, w_proj_hgrn, w_mix_out, ln2_g, ln2_b, ffn2_w_in, ffn2_w_out, ln3_g, ln3_b, ple_w_proj, ple_w_gate):
    batch, seq, d_model = x.shape
    depth = ffn1_w_in.shape[0]
    alpha = (2 * depth) ** 0.25
    width = HGRN_HEADS * HGRN_DIM
    gw = ATT_GROUP_WIDTH
    t = batch * seq
    h = x.reshape(t, d_model)
    pos_rows = positions.reshape(t // ATT_TILE, 1, ATT_TILE)
    lb_logits = hgrn_lb_logits.astype(F32)
    vec = lambda a: a.reshape(1, -1).astype(F32)
    c_hgrn = 3 * ATT_WIDTH
    c_gate = c_hgrn + 4 * width
    pair_order, _ = _pair_lane_order()
    att_cols = []
    for g in range(len(ATT_GROUPS)):
        for part in range(3):
            lo = part * ATT_WIDTH + g * gw
            slab_order = pair_order if part < 2 else list(range(LANES))
            att_cols += [lo + p * LANES + src for p in range(gw // LANES) for src in slab_order]
    att_cols = jnp.asarray(att_cols, jnp.int32)
    for i in range(depth):
        w_in = w_mix_in[i]
        w_qkv = jnp.take(w_in, att_cols, axis=1).astype(BF16)
        w_mix = w_mix_in.astype(F32)

        h = _ffn_ln(h, ffn1_w_in[i].astype(F32), ffn1_w_out[i].astype(BF16), vec(ln1_g[i]), vec(ln1_b[i]), alpha)
        for window, dilation in ATT_GROUPS:
            assert window == ATT_BLOCK * dilation
        o_att = _attention(h, pos_rows, w_qkv, batch, seq)
        o_hgrn = _hgrn_scan(h, w_mix, c_hgrn, lb_logits, vec(hgrn_norm_g[i]), i, batch, seq)
        h = _merge(h, o_att, o_hgrn, w_mix, c_gate, i, w_proj_attn[i].astype(F32), w_proj_hgrn[i].astype(F32),
                   w_mix_out[i].astype(F32), vec(ln2_g[i]), vec(ln2_b[i]), alpha)
        h = _ffn_ln(h, ffn2_w_in[i].astype(F32), ffn2_w_out[i].astype(BF16), vec(ln3_g[i]), vec(ln3_b[i]), alpha,
                    ple=(p[i].reshape(t, -1), ple_w_gate[i].astype(BF16), ple_w_proj[i].astype(BF16)))
    return h.reshape(batch, seq, d_model)
```
